```python
import math
import jax, jax.numpy as jnp
from jax import lax
import numpy as np

D_MODEL = 4096
BATCH = 2
SEQ = 4096
DEPTH = 2

HEAD_DIM = 128
N_HEADS_TOTAL = D_MODEL // HEAD_DIM
N_HEADS_A = N_HEADS_TOTAL // 2
N_KV_A = N_HEADS_A // 4
N_HEADS_B = N_HEADS_TOTAL // 2
N_KV_B = N_HEADS_B // 4
WINDOW = 128
BLOCK = 128
GRID_W = 64
ROPE_THETA = 10000.0
DIFF_DIM = 128
N_HEADS_C = D_MODEL // (2 * DIFF_DIM)
N_EXPERTS = 16
CAPACITY_FACTOR = 2
D_FF = D_MODEL // 4
EPS = 1e-6
NEG = -1e30
N_EVEN = (DEPTH + 1) // 2
N_ODD = DEPTH // 2

Q_A = N_HEADS_A * HEAD_DIM
KV_A = N_KV_A * HEAD_DIM
Q_B = N_HEADS_B * HEAD_DIM
KV_B = N_KV_B * HEAD_DIM
W_IN_EVEN = Q_A + 2 * KV_A + Q_B + 2 * KV_B
W_MIX_EVEN = Q_A + Q_B
QK_C = N_HEADS_C * 2 * DIFF_DIM
W_IN_ODD = 3 * QK_C

kernel_name = 'hybrid_window_grid_diff_ec_moe'

F32 = jnp.float32


def rmsnorm(x, g):
    xf = x.astype(F32)
    y = xf * lax.rsqrt(jnp.mean(xf * xf, axis=-1, keepdims=True) + EPS)
    return (y * g.astype(F32)).astype(x.dtype)


def alibi_slopes(n):
    return 2.0 ** (-8.0 * jnp.arange(1, n + 1, dtype=F32) / n)


def window_attention(q, k, v, sink):
    b, s, h, d = q.shape
    hkv = k.shape[2]
    g = h // hkv
    nb = s // BLOCK
    qb = q.reshape(b, nb, BLOCK, hkv, g, d).astype(F32)

    def bands(t):
        tp = jnp.pad(t.astype(F32), ((0, 0), (BLOCK, BLOCK), (0, 0), (0, 0)))
        tp = tp.reshape(b, nb + 2, BLOCK, hkv, d)
        return jnp.concatenate([tp[:, :-2], tp[:, 1:-1], tp[:, 2:]], axis=2)

    kb, vb = bands(k), bands(v)
    sc = jnp.einsum('bnqkgd,bnskd->bnkgqs', qb, kb) * (d ** -0.5)
    qi = jnp.arange(BLOCK)
    kj = jnp.arange(3 * BLOCK)
    rel = BLOCK + qi[:, None] - kj[None, :]
    kpos = (jnp.arange(nb)[:, None] - 1) * BLOCK + kj[None, :]
    in_range = (kpos >= 0) & (kpos < s)
    valid = (jnp.abs(rel) <= WINDOW)[None] & in_range[:, None, :]
    slopes = alibi_slopes(h).reshape(hkv, g, 1, 1)
    bias = -slopes * jnp.abs(rel).astype(F32)
    sc = jnp.where(valid[None, :, None, None], sc + bias, NEG)
    sk = sink.astype(F32).reshape(1, 1, hkv, g, 1, 1)
    m = jnp.maximum(jnp.max(sc, axis=-1, keepdims=True), sk)
    p = jnp.exp(sc - m)
    den = jnp.sum(p, axis=-1, keepdims=True) + jnp.exp(sk - m)
    out = jnp.einsum('bnkgqs,bnskd->bnqkgd', p / den, vb)
    return out.reshape(b, s, h, d).astype(q.dtype)


def axial_rope(x):
    s, d = x.shape[1], x.shape[-1]
    rows = s // GRID_W
    row = jnp.repeat(jnp.arange(rows), GRID_W).astype(F32)
    col = jnp.tile(jnp.arange(GRID_W), rows).astype(F32)
    half = d // 2
    quarter = half // 2
    inv = ROPE_THETA ** (-jnp.arange(quarter, dtype=F32) / quarter)

    def rot(xp, pos):
        ang = pos[:, None] * inv[None, :]
        cos = jnp.cos(ang)[None, :, None, :]
        sin = jnp.sin(ang)[None, :, None, :]
        x1, x2 = xp[..., :quarter], xp[..., quarter:]
        return jnp.concatenate([x1 * cos - x2 * sin, x2 * cos + x1 * sin], axis=-1)

    xf = x.astype(F32)
    return jnp.concatenate([rot(xf[..., :half], row), rot(xf[..., half:], col)], axis=-1).astype(x.dtype)


def grid_attention(q, k, v):
    b, s, h, d = q.shape
    hkv = k.shape[2]
    g = h // hkv
    nb = s // BLOCK
    qb = q.reshape(b, nb, BLOCK, hkv, g, d).transpose(1, 0, 2, 3, 4, 5)
    kf = k.astype(F32)
    vf = v.astype(F32)

    def one(qblk):
        sc = jnp.einsum('bqkgd,bskd->bkgqs', qblk.astype(F32), kf) * (d ** -0.5)
        p = jax.nn.softmax(sc, axis=-1)
        return jnp.einsum('bkgqs,bskd->bqkgd', p, vf)

    out = lax.map(one, qb)
    return out.transpose(1, 0, 2, 3, 4, 5).reshape(b, s, h, d).astype(q.dtype)


def diff_attention(q, k, v, lam):
    b, s, h = q.shape[0], q.shape[1], q.shape[2]
    dd = q.shape[-1]
    nb = s // BLOCK
    qb = q.reshape(b, nb, BLOCK, h, 2, dd).transpose(1, 0, 2, 3, 4, 5)
    kf = k.astype(F32)
    vf = v.astype(F32)
    slopes = alibi_slopes(h)[:, None, None, None]
    kpos = jnp.arange(s)

    def one(args):
        qblk, blk = args
        sc = jnp.einsum('bqhcd,bshcd->bhcqs', qblk.astype(F32), kf) * (dd ** -0.5)
        qpos = blk * BLOCK + jnp.arange(BLOCK)
        dist = jnp.abs(qpos[:, None] - kpos[None, :]).astype(F32)
        p = jax.nn.softmax(sc - slopes * dist, axis=-1)
        attn = p[:, :, 0] - lam * p[:, :, 1]
        return jnp.einsum('bhqs,bshe->bqhe', attn, vf)

    out = lax.map(one, (qb, jnp.arange(nb)))
    return out.transpose(1, 0, 2, 3, 4).reshape(b, s, h, 2 * dd)


def even_mixer(hn, w_in, w_out, sink_a, qnorm_b, knorm_b):
    b, s, _ = hn.shape
    proj = jnp.einsum('bsd,de->bse', hn, w_in)
    cuts = [Q_A, Q_A + KV_A, Q_A + 2 * KV_A, Q_A + 2 * KV_A + Q_B, Q_A + 2 * KV_A + Q_B + KV_B]
    qa, ka, va, qb, kb, vb = jnp.split(proj, cuts, axis=-1)
    qa = qa.reshape(b, s, N_HEADS_A, HEAD_DIM)
    ka = ka.reshape(b, s, N_KV_A, HEAD_DIM)
    va = va.reshape(b, s, N_KV_A, HEAD_DIM)
    qb = axial_rope(rmsnorm(qb.reshape(b, s, N_HEADS_B, HEAD_DIM), qnorm_b))
    kb = axial_rope(rmsnorm(kb.reshape(b, s, N_KV_B, HEAD_DIM), knorm_b))
    vb = vb.reshape(b, s, N_KV_B, HEAD_DIM)
    out_a = window_attention(qa, ka, va, sink_a)
    out_b = grid_attention(qb, kb, vb)
    mixed = jnp.concatenate([out_a, out_b], axis=2).reshape(b, s, W_MIX_EVEN)
    return jnp.einsum('bse,ed->bsd', mixed, w_out)


def odd_mixer(hn, w_in, w_out, c_lambda, c_subnorm, lambda_init):
    b, s, _ = hn.shape
    proj = jnp.einsum('bsd,de->bse', hn, w_in)
    q, k, v = jnp.split(proj, [QK_C, 2 * QK_C], axis=-1)
    q = q.reshape(b, s, N_HEADS_C, 2, DIFF_DIM)
    k = k.reshape(b, s, N_HEADS_C, 2, DIFF_DIM)
    v = v.reshape(b, s, N_HEADS_C, 2 * DIFF_DIM)
    lf = c_lambda.astype(F32)
    lam = jnp.exp(jnp.sum(lf[0] * lf[1])) - jnp.exp(jnp.sum(lf[2] * lf[3])) + lambda_init
    out = diff_attention(q, k, v, lam)
    out = rmsnorm(out, c_subnorm) * (1.0 - lambda_init)
    out = out.reshape(b, s, N_HEADS_C * 2 * DIFF_DIM).astype(hn.dtype)
    return jnp.einsum('bse,ed->bsd', out, w_out)


def expert_choice_ffn(xn, w_router, w_gate, w_up, w_down):
    b, s, _ = xn.shape
    cap = CAPACITY_FACTOR * s // N_EXPERTS
    logits = jnp.einsum('bsd,de->bse', xn.astype(F32), w_router.astype(F32))
    aff = jax.nn.softmax(logits, axis=-1)
    gates, idx = lax.top_k(aff.transpose(0, 2, 1), cap)
    bidx = jnp.arange(b)[:, None, None]
    xg = xn[bidx, idx]
    hid = jax.nn.silu(jnp.einsum('becd,edf->becf', xg, w_gate)) * jnp.einsum('becd,edf->becf', xg, w_up)
    y = jnp.einsum('becf,efd->becd', hid, w_down) * gates[..., None].astype(xn.dtype)
    return jnp.zeros_like(xn).at[bidx, idx].add(y)


def setup_inputs(seed: int = 0) -> dict:
    key = jax.random.key(seed)
    ks = jax.random.split(key, 17)

    def nrm(k, shape, scale):
        return jax.random.normal(k, shape, F32) * scale

    def gain(k, shape):
        return 1.0 + 0.05 * jax.random.normal(k, shape, F32)

    return {
        'x': nrm(ks[0], (BATCH, SEQ, D_MODEL), 1.0),
        'norm_mix': gain(ks[1], (DEPTH, D_MODEL)),
        'norm_ffn': gain(ks[2], (DEPTH, D_MODEL)),
        'norm_final': gain(ks[3], (D_MODEL,)),
        'w_in_even': nrm(ks[4], (N_EVEN, D_MODEL, W_IN_EVEN), D_MODEL ** -0.5),
        'w_out_even': nrm(ks[5], (N_EVEN, W_MIX_EVEN, D_MODEL), W_MIX_EVEN ** -0.5),
        'sink_a': nrm(ks[6], (N_EVEN, N_HEADS_A), 0.5),
        'qnorm_b': gain(ks[7], (N_EVEN, HEAD_DIM)),
        'knorm_b': gain(ks[8], (N_EVEN, HEAD_DIM)),
        'w_in_odd': nrm(ks[9], (N_ODD, D_MODEL, W_IN_ODD), D_MODEL ** -0.5),
        'w_out_odd': nrm(ks[10], (N_ODD, QK_C, D_MODEL), QK_C ** -0.5),
        'c_lambda': nrm(ks[11], (N_ODD, 4, DIFF_DIM), 0.1),
        'c_subnorm': gain(ks[12], (N_ODD, 2 * DIFF_DIM)),
        'w_router': nrm(ks[13], (DEPTH, D_MODEL, N_EXPERTS), D_MODEL ** -0.5),
        'w_gate': nrm(ks[14], (DEPTH, N_EXPERTS, D_MODEL, D_FF), D_MODEL ** -0.5),
        'w_up': nrm(ks[15], (DEPTH, N_EXPERTS, D_MODEL, D_FF), D_MODEL ** -0.5),
        'w_down': nrm(ks[16], (DEPTH, N_EXPERTS, D_FF, D_MODEL), D_FF ** -0.5),
    }


def reference(x, norm_mix, norm_ffn, norm_final, w_in_even, w_out_even, sink_a, qnorm_b, knorm_b,
              w_in_odd, w_out_odd, c_lambda, c_subnorm, w_router, w_gate, w_up, w_down):
    h = x
    for layer in range(DEPTH):
        i = layer // 2
        hn = rmsnorm(h, norm_mix[layer])
        if layer % 2 == 0:
            mix = even_mixer(hn, w_in_even[i], w_out_even[i], sink_a[i], qnorm_b[i], knorm_b[i])
        else:
            lambda_init = 0.8 - 0.6 * math.exp(-0.3 * layer)
            mix = odd_mixer(hn, w_in_odd[i], w_out_odd[i], c_lambda[i], c_subnorm[i], lambda_init)
        h = h + mix
        h = h + expert_choice_ffn(rmsnorm(h, norm_ffn[layer]), w_router[layer], w_gate[layer],
                                  w_up[layer], w_down[layer])
    return rmsnorm(h, norm_final)
```

```python
import functools
import math

import jax
import jax.numpy as jnp
from jax import lax
from jax.experimental import pallas as pl
from jax.experimental.pallas import tpu as pltpu

F32 = jnp.float32
BF16 = jnp.bfloat16

HEAD_DIM = 128
WINDOW = 128
BLOCK = 128
GRID_W = 64
ROPE_THETA = 10000.0
DIFF_DIM = 128
N_EXPERTS = 16
CAPACITY_FACTOR = 2
EPS = 1e-6
NEG = -1e30

VMEM_LIMIT_BYTES = 56 * 1024 * 1024

_NT = (((1,), (1,)), ((), ()))


def _params(*sem):
    return pltpu.CompilerParams(dimension_semantics=sem, vmem_limit_bytes=VMEM_LIMIT_BYTES)


def _rmsnorm_body(x_ref, g_ref, o_ref):
    x = x_ref[...]
    ms = jnp.mean(x * x, axis=-1, keepdims=True)
    o_ref[...] = (x * lax.rsqrt(ms + EPS) * g_ref[...]).astype(o_ref.dtype)


def _rmsnorm(x2d, g, out_dtype, tm=256):
    t, d = x2d.shape
    return pl.pallas_call(
        _rmsnorm_body,
        grid=(t // tm,),
        in_specs=[pl.BlockSpec((tm, d), lambda i: (i, 0)),
                  pl.BlockSpec((1, d), lambda i: (0, 0))],
        out_specs=pl.BlockSpec((tm, d), lambda i: (i, 0)),
        out_shape=jax.ShapeDtypeStruct((t, d), out_dtype),
        compiler_params=_params("parallel"),
        name="rmsnorm",
    )(x2d, g.reshape(1, d))


def _mm_body(*refs, n_pairs, has_res):
    o_ref = refs[-1]
    acc = None
    for p in range(n_pairs):
        d = jnp.dot(refs[2 * p][...], refs[2 * p + 1][...], preferred_element_type=F32)
        acc = d if acc is None else acc + d
    if has_res:
        acc = acc + refs[2 * n_pairs][...]
    o_ref[...] = acc.astype(o_ref.dtype)


def _matmul(a_list, w, res, out_dtype, tm, tn, name):
    m = a_list[0].shape[0]
    n = w.shape[1]
    in_specs, args = [], []
    kp = a_list[0].shape[1]
    assert all(a.shape == (m, kp) for a in a_list) and w.shape[0] == kp * len(a_list)
    for part, a in enumerate(a_list):
        in_specs.append(pl.BlockSpec((tm, kp), lambda j, i: (i, 0)))
        in_specs.append(pl.BlockSpec((kp, tn), functools.partial(lambda j, i, r: (r, j), r=part)))
        args += [a, w]
    if res is not None:
        in_specs.append(pl.BlockSpec((tm, tn), lambda j, i: (i, j)))
        args.append(res)
    return pl.pallas_call(
        functools.partial(_mm_body, n_pairs=len(a_list), has_res=res is not None),
        grid=(n // tn, m // tm),
        in_specs=in_specs,
        out_specs=pl.BlockSpec((tm, tn), lambda j, i: (i, j)),
        out_shape=jax.ShapeDtypeStruct((m, n), out_dtype),
        compiler_params=_params("parallel", "arbitrary"),
        name=name,
    )(*args)


def _win_body(slope_ref, sink_ref, q_ref, kp_ref, kc_ref, kn_ref, vp_ref, vc_ref, vn_ref, o_ref,
              *, nb, group, scale):
    kv = pl.program_id(1)
    n = pl.program_id(2)
    kb = jnp.concatenate([kp_ref[0], kc_ref[0], kn_ref[0]], axis=0)
    vb = jnp.concatenate([vp_ref[0], vc_ref[0], vn_ref[0]], axis=0)
    qi = lax.broadcasted_iota(jnp.int32, (BLOCK, 3 * BLOCK), 0)
    kj = lax.broadcasted_iota(jnp.int32, (BLOCK, 3 * BLOCK), 1)
    arel = jnp.abs(BLOCK + qi - kj)
    kpos = (n - 1) * BLOCK + kj
    valid = (arel <= WINDOW) & (kpos >= 0) & (kpos < nb * BLOCK)
    arel_f = arel.astype(F32)
    q = q_ref[0]
    for g in range(group):
        h = kv * group + g
        sk = sink_ref[h]
        s = lax.dot_general(q[:, g * HEAD_DIM:(g + 1) * HEAD_DIM], kb, _NT,
                            preferred_element_type=F32) * scale
        s = jnp.where(valid, s - slope_ref[h] * arel_f, NEG)
        m = jnp.maximum(jnp.max(s, axis=-1, keepdims=True), sk)
        p = jnp.exp(s - m)
        den = jnp.sum(p, axis=-1, keepdims=True) + jnp.exp(sk - m)
        o = jnp.dot(p.astype(BF16), vb, preferred_element_type=F32) / den
        o_ref[0, :, g * HEAD_DIM:(g + 1) * HEAD_DIM] = o.astype(o_ref.dtype)


def _window_attention(proj, slopes, sink, *, n_heads, n_kv, q_col, k_col, v_col):
    b, s, _ = proj.shape
    nb = s // BLOCK
    group = n_heads // n_kv
    qw = group * HEAD_DIM
    qb0, kb0, vb0 = q_col // qw, k_col // HEAD_DIM, v_col // HEAD_DIM
    assert q_col % qw == 0 and k_col % HEAD_DIM == 0 and v_col % HEAD_DIM == 0

    def kv_spec(col0, shift):
        def imap(bi, kv, n):
            return (bi, jnp.clip(n + shift, 0, nb - 1), col0 + kv)
        return pl.BlockSpec((1, BLOCK, HEAD_DIM), imap)

    smem = pl.BlockSpec(memory_space=pltpu.SMEM)
    return pl.pallas_call(
        functools.partial(_win_body, nb=nb, group=group, scale=HEAD_DIM ** -0.5),
        grid=(b, n_kv, nb),
        in_specs=[smem, smem,
                  pl.BlockSpec((1, BLOCK, qw), lambda bi, kv, n: (bi, n, qb0 + kv)),
                  kv_spec(kb0, -1), kv_spec(kb0, 0), kv_spec(kb0, 1),
                  kv_spec(vb0, -1), kv_spec(vb0, 0), kv_spec(vb0, 1)],
        out_specs=pl.BlockSpec((1, BLOCK, qw), lambda bi, kv, n: (bi, n, kv)),
        out_shape=jax.ShapeDtypeStruct((b, s, n_heads * HEAD_DIM), BF16),
        compiler_params=_params("parallel", "parallel", "parallel"),
        name="window_attn",
    )(slopes, sink, proj, proj, proj, proj, proj, proj, proj)


def _qkprep_body(x_ref, g_ref, cos_ref, sin_ref, o_ref, *, n_qchunks, heads_per_chunk, scale):
    j = pl.program_id(2)
    is_k = j >= n_qchunks
    gain = jnp.where(is_k, g_ref[1:2, :], g_ref[0:1, :])
    mult = jnp.where(is_k, 1.0, scale)
    cos = cos_ref[...]
    sin = sin_ref[...]
    lane = lax.broadcasted_iota(jnp.int32, cos.shape, 1)
    first = (lane % (HEAD_DIM // 2)) < (HEAD_DIM // 4)
    for hd in range(heads_per_chunk):
        x = x_ref[0, :, hd * HEAD_DIM:(hd + 1) * HEAD_DIM].astype(F32)
        ms = jnp.mean(x * x, axis=-1, keepdims=True)
        y = x * lax.rsqrt(ms + EPS) * gain
        partner = jnp.where(first, pltpu.roll(y, HEAD_DIM - HEAD_DIM // 4, 1),
                            pltpu.roll(y, HEAD_DIM // 4, 1))
        r = y * cos + partner * sin
        o_ref[0, :, hd * HEAD_DIM:(hd + 1) * HEAD_DIM] = (r * mult).astype(o_ref.dtype)


def _rope_tables(s):
    rows = s // GRID_W
    row = jnp.repeat(jnp.arange(rows), GRID_W).astype(F32)
    col = jnp.tile(jnp.arange(GRID_W), rows).astype(F32)
    quarter = HEAD_DIM // 4
    inv = ROPE_THETA ** (-jnp.arange(quarter, dtype=F32) / quarter)
    ang_r = row[:, None] * inv[None, :]
    ang_c = col[:, None] * inv[None, :]
    cos = jnp.concatenate([jnp.cos(ang_r)] * 2 + [jnp.cos(ang_c)] * 2, axis=-1)
    sin = jnp.concatenate([-jnp.sin(ang_r), jnp.sin(ang_r), -jnp.sin(ang_c), jnp.sin(ang_c)], axis=-1)
    return cos, sin


def _qk_prep(proj, gains, *, q_col, n_q_heads, n_k_heads, ts=256):
    b, s, _ = proj.shape
    cw = n_k_heads * HEAD_DIM
    assert q_col % cw == 0 and (n_q_heads * HEAD_DIM) % cw == 0
    n_qchunks = n_q_heads * HEAD_DIM // cw
    c0 = q_col // cw
    cos, sin = _rope_tables(s)
    return pl.pallas_call(
        functools.partial(_qkprep_body, n_qchunks=n_qchunks, heads_per_chunk=n_k_heads,
                          scale=HEAD_DIM ** -0.5),
        grid=(b, s // ts, n_qchunks + 1),
        in_specs=[pl.BlockSpec((1, ts, cw), lambda bi, i, j: (bi, i, c0 + j)),
                  pl.BlockSpec((2, HEAD_DIM), lambda bi, i, j: (0, 0)),
                  pl.BlockSpec((ts, HEAD_DIM), lambda bi, i, j: (i, 0)),
                  pl.BlockSpec((ts, HEAD_DIM), lambda bi, i, j: (i, 0))],
        out_specs=pl.BlockSpec((1, ts, cw), lambda bi, i, j: (bi, i, j)),
        out_shape=jax.ShapeDtypeStruct((b, s, (n_qchunks + 1) * cw), BF16),
        compiler_params=_params("parallel", "parallel", "parallel"),
        name="qk_prep",
    )(proj, gains, cos, sin)


def _grid_attn_body(q_ref, k_ref, v_ref, o_ref, *, tq, group, kc, s_len):
    q = q_ref[0]
    qs = jnp.concatenate([q[:, g * HEAD_DIM:(g + 1) * HEAD_DIM] for g in range(group)], axis=0)
    rows = group * tq

    def step(j, carry):
        m, l, acc = carry
        off = pl.multiple_of(j * kc, kc)
        kch = k_ref[0, pl.ds(off, kc), :]
        vch = v_ref[0, pl.ds(off, kc), :]
        s = lax.dot_general(qs, kch, _NT, preferred_element_type=F32)
        m_new = jnp.maximum(m, jnp.max(s, axis=-1, keepdims=True))
        alpha = jnp.exp(m - m_new)
        p = jnp.exp(s - m_new)
        l = alpha * l + jnp.sum(p, axis=-1, keepdims=True)
        acc = alpha * acc + jnp.dot(p.astype(BF16), vch, preferred_element_type=F32)
        return m_new, l, acc

    init = (jnp.full((rows, 1), NEG, F32), jnp.zeros((rows, 1), F32), jnp.zeros((rows, HEAD_DIM), F32))
    _, l, acc = lax.fori_loop(0, s_len // kc, step, init)
    o = acc / l
    for g in range(group):
        o_ref[0, :, g * HEAD_DIM:(g + 1) * HEAD_DIM] = o[g * tq:(g + 1) * tq].astype(o_ref.dtype)


def _grid_attention(qk, proj, *, n_heads, n_kv, v_col, tq=128, kc=512):
    b, s, _ = qk.shape
    group = n_heads // n_kv
    qw = group * HEAD_DIM
    kc = min(kc, s)
    vb0 = v_col // HEAD_DIM
    return pl.pallas_call(
        functools.partial(_grid_attn_body, tq=tq, group=group, kc=kc, s_len=s),
        grid=(b, n_kv, s // tq),
        in_specs=[pl.BlockSpec((1, tq, qw), lambda bi, kv, i: (bi, i, kv)),
                  pl.BlockSpec((1, s, HEAD_DIM), lambda bi, kv, i: (bi, 0, n_heads + kv)),
                  pl.BlockSpec((1, s, HEAD_DIM), lambda bi, kv, i: (bi, 0, vb0 + kv))],
        out_specs=pl.BlockSpec((1, tq, qw), lambda bi, kv, i: (bi, i, kv)),
        out_shape=jax.ShapeDtypeStruct((b, s, n_heads * HEAD_DIM), BF16),
        compiler_params=_params("parallel", "parallel", "arbitrary"),
        name="grid_attn",
    )(qk, qk, proj)


def _diff_attn_body(slope_ref, lamc_ref, sub_ref, q_ref, k_ref, v_ref, o_ref,
                    *, tq, kc, s_len, scale, lambda_init):
    h = pl.program_id(1)
    i = pl.program_id(2)
    slope = slope_ref[h]
    q = q_ref[0]
    q1, q2 = q[:, :DIFF_DIM], q[:, DIFF_DIM:]
    qpos = i * tq + lax.broadcasted_iota(jnp.int32, (tq, kc), 0)
    kj = lax.broadcasted_iota(jnp.int32, (tq, kc), 1)

    def softmax_step(qc, kch, vch, bias, m, l, acc):
        s = lax.dot_general(qc, kch, _NT, preferred_element_type=F32) * scale + bias
        m_new = jnp.maximum(m, jnp.max(s, axis=-1, keepdims=True))
        alpha = jnp.exp(m - m_new)
        p = jnp.exp(s - m_new)
        l = alpha * l + jnp.sum(p, axis=-1, keepdims=True)
        acc = alpha * acc + jnp.dot(p.astype(BF16), vch, preferred_element_type=F32)
        return m_new, l, acc

    def step(j, carry):
        m1, l1, a1, m2, l2, a2 = carry
        off = pl.multiple_of(j * kc, kc)
        kch = k_ref[0, pl.ds(off, kc), :]
        vch = v_ref[0, pl.ds(off, kc), :]
        bias = -slope * jnp.abs(qpos - (off + kj)).astype(F32)
        m1, l1, a1 = softmax_step(q1, kch[:, :DIFF_DIM], vch, bias, m1, l1, a1)
        m2, l2, a2 = softmax_step(q2, kch[:, DIFF_DIM:], vch, bias, m2, l2, a2)
        return m1, l1, a1, m2, l2, a2

    def init():
        return (jnp.full((tq, 1), NEG, F32), jnp.zeros((tq, 1), F32), jnp.zeros((tq, 2 * DIFF_DIM), F32))

    _, l1, a1, _, l2, a2 = lax.fori_loop(0, s_len // kc, step, init() + init())
    lf = lamc_ref[...]
    lam = (jnp.exp(jnp.sum(lf[0:1] * lf[1:2], axis=-1, keepdims=True))
           - jnp.exp(jnp.sum(lf[2:3] * lf[3:4], axis=-1, keepdims=True)) + lambda_init)
    o = a1 / l1 - lam * (a2 / l2)
    ms = jnp.mean(o * o, axis=-1, keepdims=True)
    o = o * lax.rsqrt(ms + EPS) * sub_ref[...] * (1.0 - lambda_init)
    o_ref[0] = o.astype(o_ref.dtype)


def _diff_attention(proj, slopes, c_lambda, c_subnorm, lambda_init, *, n_heads, tq=256, kc=512):
    b, s, _ = proj.shape
    hw = 2 * DIFF_DIM
    kc = min(kc, s)
    tq = min(tq, s)
    return pl.pallas_call(
        functools.partial(_diff_attn_body, tq=tq, kc=kc, s_len=s, scale=DIFF_DIM ** -0.5,
                          lambda_init=lambda_init),
        grid=(b, n_heads, s // tq),
        in_specs=[pl.BlockSpec(memory_space=pltpu.SMEM),
                  pl.BlockSpec((4, DIFF_DIM), lambda bi, h, i: (0, 0)),
                  pl.BlockSpec((1, hw), lambda bi, h, i: (0, 0)),
                  pl.BlockSpec((1, tq, hw), lambda bi, h, i: (bi, i, h)),
                  pl.BlockSpec((1, s, hw), lambda bi, h, i: (bi, 0, n_heads + h)),
                  pl.BlockSpec((1, s, hw), lambda bi, h, i: (bi, 0, 2 * n_heads + h))],
        out_specs=pl.BlockSpec((1, tq, hw), lambda bi, h, i: (bi, i, h)),
        out_shape=jax.ShapeDtypeStruct((b, s, n_heads * hw), BF16),
        compiler_params=_params("parallel", "parallel", "arbitrary"),
        name="diff_attn",
    )(slopes, c_lambda, c_subnorm.reshape(1, hw), proj, proj, proj)


def _router_body(h_ref, g_ref, wrt_ref, aff_ref):
    x = h_ref[...]
    ms = jnp.mean(x * x, axis=-1, keepdims=True)
    xn = x * lax.rsqrt(ms + EPS) * g_ref[...]
    logits = lax.dot_general(wrt_ref[...], xn, _NT, precision=lax.Precision.HIGHEST,
                             preferred_element_type=F32)
    m = jnp.max(logits, axis=0, keepdims=True)
    e = jnp.exp(logits - m)
    aff_ref[0] = e / jnp.sum(e, axis=0, keepdims=True)


def _router(h2d, g, w_router, b, s, tm=512):
    t, d = h2d.shape
    tm = min(tm, s)
    spb = s // tm
    ne = w_router.shape[1]
    return pl.pallas_call(
        _router_body,
        grid=(t // tm,),
        in_specs=[pl.BlockSpec((tm, d), lambda i: (i, 0)),
                  pl.BlockSpec((1, d), lambda i: (0, 0)),
                  pl.BlockSpec((ne, d), lambda i: (0, 0))],
        out_specs=pl.BlockSpec((1, ne, tm), lambda i: (i // spb, 0, i % spb)),
        out_shape=jax.ShapeDtypeStruct((b, ne, s), F32),
        compiler_params=_params("parallel"),
        name="router",
    )(h2d, g.reshape(1, d), w_router.T)


def _cumsum_lanes(mask, tri):
    rows, n = mask.shape
    carry = jnp.zeros((rows, 1), F32)
    out = []
    for blk in range(n // 128):
        c = jnp.dot(mask[:, blk * 128:(blk + 1) * 128].astype(BF16), tri, preferred_element_type=F32) + carry
        out.append(c)
        carry = c[:, 127:128]
    return jnp.concatenate(out, axis=1)


def _select_body(aff_ref, pos_ref, *, cap):
    a = aff_ref[0]
    bits = pltpu.bitcast(a, jnp.int32)
    ne = a.shape[0]
    thr = jnp.zeros((ne, 1), jnp.int32)
    for bit in range(30, -1, -1):
        cand = thr | (1 << bit)
        cnt = jnp.sum((bits >= cand).astype(jnp.int32), axis=1, keepdims=True)
        thr = jnp.where(cnt >= cap, cand, thr)
    gt = bits > thr
    eq = bits == thr
    need = (cap - jnp.sum(gt.astype(jnp.int32), axis=1, keepdims=True)).astype(F32)
    r = lax.broadcasted_iota(jnp.int32, (128, 128), 0)
    c = lax.broadcasted_iota(jnp.int32, (128, 128), 1)
    tri = (r <= c).astype(BF16)
    sel = gt | (eq & (_cumsum_lanes(eq.astype(F32), tri) <= need))
    pos = _cumsum_lanes(sel.astype(F32), tri) - 1.0
    pos_ref[0] = jnp.where(sel, pos, -1.0)


def _select(aff, cap):
    b, ne, s = aff.shape
    return pl.pallas_call(
        functools.partial(_select_body, cap=cap),
        grid=(b,),
        in_specs=[pl.BlockSpec((1, ne, s), lambda bi: (bi, 0, 0))],
        out_specs=pl.BlockSpec((1, ne, s), lambda bi: (bi, 0, 0)),
        out_shape=jax.ShapeDtypeStruct((b, ne, s), F32),
        compiler_params=_params("parallel"),
        name="select",
    )(aff)


def _compact_body(pos_ref, aff_ref, idx_ref, gate_ref, *, tc):
    c0 = pl.program_id(1) * tc
    pos = pos_ref[0]
    s = pos.shape[1]
    slot = (c0 + lax.broadcasted_iota(jnp.int32, (tc, s), 0)).astype(F32)
    tok = lax.broadcasted_iota(jnp.int32, (tc, s), 1)
    hit = pos == slot
    idx_ref[0] = jnp.sum(jnp.where(hit, tok, 0), axis=1, keepdims=True)
    gate_ref[0] = jnp.sum(jnp.where(hit, aff_ref[0], 0.0), axis=1, keepdims=True)


def _compact(pos, aff, cap, tc=128):
    b, ne, s = pos.shape
    tc = min(tc, cap)
    row = pl.BlockSpec((1, 1, s), lambda r, j: (r, 0, 0))
    col = pl.BlockSpec((1, tc, 1), lambda r, j: (r, j, 0))
    return pl.pallas_call(
        functools.partial(_compact_body, tc=tc),
        grid=(b * ne, cap // tc),
        in_specs=[row, row],
        out_specs=[col, col],
        out_shape=[jax.ShapeDtypeStruct((b * ne, cap, 1), jnp.int32),
                   jax.ShapeDtypeStruct((b * ne, cap, 1), F32)],
        compiler_params=_params("parallel", "parallel"),
        name="compact",
    )(pos.reshape(b * ne, 1, s), aff.reshape(b * ne, 1, s))


def _ffn_body(idx_ref, h_hbm, g_ref, gate_ref, wg_ref, wu_ref, wd_ref, o_ref, xg, sem,
              *, cap, s_len, n_f):
    e = pl.program_id(0)
    b = pl.program_id(1)
    f = pl.program_id(2)
    ne = pl.num_programs(0)
    hbuf = o_ref.at[0]

    def row_copy(src_row, c):
        return pltpu.make_async_copy(h_hbm.at[pl.ds(src_row, 1)], hbuf.at[pl.ds(c, 1)], sem)

    @pl.when(f == 0)
    def _gather_and_norm():
        base = (b * ne + e) * cap

        def issue(c, carry):
            row_copy(b * s_len + idx_ref[base + c], c).start()
            return carry

        def drain(c, carry):
            row_copy(0, c).wait()
            return carry

        lax.fori_loop(0, cap, issue, 0)
        lax.fori_loop(0, cap, drain, 0)
        x = hbuf[...]
        ms = jnp.mean(x * x, axis=-1, keepdims=True)
        xg[...] = (x * lax.rsqrt(ms + EPS) * g_ref[...]).astype(BF16)

    x = xg[...]
    gt = jnp.dot(x, wg_ref[0], preferred_element_type=F32)
    up = jnp.dot(x, wu_ref[0], preferred_element_type=F32)
    hid = (gt * jax.nn.sigmoid(gt) * up).astype(BF16)
    y = jnp.dot(hid, wd_ref[0], preferred_element_type=F32)

    @pl.when(f == 0)
    def _first():
        o_ref[0] = y

    @pl.when(f > 0)
    def _rest():
        o_ref[0] += y

    @pl.when(f == n_f - 1)
    def _gate():
        o_ref[0] = o_ref[0] * gate_ref[0]


def _expert_ffn(idx_flat, h2d, g, gates, w_gate, w_up, w_down, b, s, cap, tf=256):
    t, d = h2d.shape
    ne, _, dff = w_gate.shape
    tf = min(tf, dff)
    n_f = dff // tf
    grid_spec = pltpu.PrefetchScalarGridSpec(
        num_scalar_prefetch=1,
        grid=(ne, b, n_f),
        in_specs=[pl.BlockSpec(memory_space=pl.ANY),
                  pl.BlockSpec((1, d), lambda e, bi, f, idx: (0, 0)),
                  pl.BlockSpec((1, cap, 1), lambda e, bi, f, idx: (bi * ne + e, 0, 0)),
                  pl.BlockSpec((1, d, tf), lambda e, bi, f, idx: (e, 0, f)),
                  pl.BlockSpec((1, d, tf), lambda e, bi, f, idx: (e, 0, f)),
                  pl.BlockSpec((1, tf, d), lambda e, bi, f, idx: (e, f, 0))],
        out_specs=pl.BlockSpec((1, cap, d), lambda e, bi, f, idx: (bi * ne + e, 0, 0)),
        scratch_shapes=[pltpu.VMEM((cap, d), BF16), pltpu.SemaphoreType.DMA(())],
    )
    return pl.pallas_call(
        functools.partial(_ffn_body, cap=cap, s_len=s, n_f=n_f),
        grid_spec=grid_spec,
        out_shape=jax.ShapeDtypeStruct((b * ne, cap, d), F32),
        compiler_params=_params("arbitrary", "arbitrary", "arbitrary"),
        name="expert_ffn",
    )(idx_flat, h2d, g.reshape(1, d), gates, w_gate, w_up, w_down)


def _combine_body(idx_ref, h_ref, y_ref, o_ref, *, cap, ne, rows_per_group):
    b = pl.program_id(0)
    e = pl.program_id(2)

    @pl.when(e == 0)
    def _init():
        o_ref[...] = h_ref[...]

    base = (b * ne + e) * cap

    def group(gi, carry):
        c0 = pl.multiple_of(gi * rows_per_group, rows_per_group)
        toks = [idx_ref[base + c0 + r] for r in range(rows_per_group)]
        cur = [o_ref[0, pl.ds(toks[r], 1), :] for r in range(rows_per_group)]
        upd = [cur[r] + y_ref[0, pl.ds(c0 + r, 1), :] for r in range(rows_per_group)]
        for r in range(rows_per_group):
            o_ref[0, pl.ds(toks[r], 1), :] = upd[r]
        return carry

    lax.fori_loop(0, cap // rows_per_group, group, 0)


def _combine(idx_flat, h, y, cap, tn=256, rows_per_group=16):
    b, s, d = h.shape
    ne = y.shape[0] // b
    tn = min(tn, d)
    rows_per_group = min(rows_per_group, cap)
    grid_spec = pltpu.PrefetchScalarGridSpec(
        num_scalar_prefetch=1,
        grid=(b, d // tn, ne),
        in_specs=[pl.BlockSpec((1, s, tn), lambda bi, j, e, idx: (bi, 0, j)),
                  pl.BlockSpec((1, cap, tn), lambda bi, j, e, idx: (bi * ne + e, 0, j))],
        out_specs=pl.BlockSpec((1, s, tn), lambda bi, j, e, idx: (bi, 0, j)),
    )
    return pl.pallas_call(
        functools.partial(_combine_body, cap=cap, ne=ne, rows_per_group=rows_per_group),
        grid_spec=grid_spec,
        out_shape=jax.ShapeDtypeStruct((b, s, d), F32),
        compiler_params=_params("parallel", "parallel", "arbitrary"),
        name="combine",
    )(idx_flat, h, y)


def _moe(h, g, w_router, w_gate, w_up, w_down):
    b, s, d = h.shape
    ne = w_router.shape[1]
    cap = CAPACITY_FACTOR * s // ne
    h2d = h.reshape(b * s, d)
    aff = _router(h2d, g, w_router, b, s)
    pos = _select(aff, cap)
    idx, gates = _compact(pos, aff, cap)
    idx_flat = idx.reshape(b * ne * cap)
    y = _expert_ffn(idx_flat, h2d, g, gates, w_gate.astype(BF16), w_up.astype(BF16),
                    w_down.astype(BF16), b, s, cap)
    return _combine(idx_flat, h, y, cap)


def _alibi_slopes(n):
    return 2.0 ** (-8.0 * jnp.arange(1, n + 1, dtype=F32) / n)


def _mm_tiles(m, n):
    tm = next(t for t in (512, 256, 128, 8) if m % t == 0)
    tn = next(t for t in (1024, 512, 256, 128) if n % t == 0)
    return tm, tn


def _even_mixer(h, g, w_in, w_out, sink, qnorm, knorm):
    b, s, d = h.shape
    t = b * s
    n_heads = d // HEAD_DIM
    ha, hb = n_heads // 2, n_heads // 2
    kva, kvb = ha // 4, hb // 4
    q_a, kv_a, q_b, kv_b = ha * HEAD_DIM, kva * HEAD_DIM, hb * HEAD_DIM, kvb * HEAD_DIM
    h2d = h.reshape(t, d)
    hn = _rmsnorm(h2d, g, BF16)
    w_total = w_in.shape[1]
    tm, tn = _mm_tiles(t, w_total)
    proj = _matmul([hn], w_in.astype(BF16), None, BF16, tm, tn, "in_proj_even").reshape(b, s, w_total)
    out_a = _window_attention(proj, _alibi_slopes(ha), sink, n_heads=ha, n_kv=kva,
                              q_col=0, k_col=q_a, v_col=q_a + kv_a)
    qb0 = q_a + 2 * kv_a
    qk = _qk_prep(proj, jnp.stack([qnorm, knorm]), q_col=qb0, n_q_heads=hb, n_k_heads=kvb)
    out_b = _grid_attention(qk, proj, n_heads=hb, n_kv=kvb, v_col=qb0 + q_b + kv_b)
    tm, tn = _mm_tiles(t, d)
    out = _matmul([out_a.reshape(t, q_a), out_b.reshape(t, q_b)], w_out.astype(BF16), h2d, F32,
                  tm, tn, "out_proj_even")
    return out.reshape(b, s, d)


def _odd_mixer(h, g, w_in, w_out, c_lambda, c_subnorm, lambda_init):
    b, s, d = h.shape
    t = b * s
    n_heads = d // (2 * DIFF_DIM)
    h2d = h.reshape(t, d)
    hn = _rmsnorm(h2d, g, BF16)
    w_total = w_in.shape[1]
    tm, tn = _mm_tiles(t, w_total)
    proj = _matmul([hn], w_in.astype(BF16), None, BF16, tm, tn, "in_proj_odd").reshape(b, s, w_total)
    mix = _diff_attention(proj, _alibi_slopes(n_heads), c_lambda, c_subnorm, lambda_init, n_heads=n_heads)
    tm, tn = _mm_tiles(t, d)
    out = _matmul([mix.reshape(t, d)], w_out.astype(BF16), h2d, F32, tm, tn, "out_proj_odd")
    return out.reshape(b, s, d)


def kernel(x, norm_mix, norm_ffn, norm_final, w_in_even, w_out_even, sink_a, qnorm_b, knorm_b,
           w_in_odd, w_out_odd, c_lambda, c_subnorm, w_router, w_gate, w_up, w_down):
    b, s, d = x.shape
    depth = norm_mix.shape[0]
    h = x
    for layer in range(depth):
        i = layer // 2
        if layer % 2 == 0:
            h = _even_mixer(h, norm_mix[layer], w_in_even[i], w_out_even[i], sink_a[i], qnorm_b[i], knorm_b[i])
        else:
            lambda_init = 0.8 - 0.6 * math.exp(-0.3 * layer)
            h = _odd_mixer(h, norm_mix[layer], w_in_odd[i], w_out_odd[i], c_lambda[i], c_subnorm[i], lambda_init)
        h = _moe(h, norm_ffn[layer], w_router[layer], w_gate[layer], w_up[layer], w_down[layer])
    return _rmsnorm(h.reshape(b * s, d), norm_final, F32).reshape(b, s, d)
```

```python
import functools
import math

import jax
import jax.numpy as jnp
from jax import lax
from jax.experimental import pallas as pl
from jax.experimental.pallas import tpu as pltpu

F32 = jnp.float32
BF16 = jnp.bfloat16

HEAD_DIM = 128
WINDOW = 128
BLOCK = 128
GRID_W = 64
ROPE_THETA = 10000.0
DIFF_DIM = 128
N_EXPERTS = 16
CAPACITY_FACTOR = 2
EPS = 1e-6
NEG = -1e30

VMEM_LIMIT_BYTES = 56 * 1024 * 1024

_NT = (((1,), (1,)), ((), ()))


def _params(*sem):
    return pltpu.CompilerParams(dimension_semantics=sem, vmem_limit_bytes=VMEM_LIMIT_BYTES)


def _rmsnorm_body(x_ref, g_ref, o_ref):
    x = x_ref[...]
    ms = jnp.mean(x * x, axis=-1, keepdims=True)
    o_ref[...] = (x * lax.rsqrt(ms + EPS) * g_ref[...]).astype(o_ref.dtype)


def _rmsnorm(x2d, g, out_dtype, tm=256):
    t, d = x2d.shape
    return pl.pallas_call(
        _rmsnorm_body,
        grid=(t // tm,),
        in_specs=[pl.BlockSpec((tm, d), lambda i: (i, 0)),
                  pl.BlockSpec((1, d), lambda i: (0, 0))],
        out_specs=pl.BlockSpec((tm, d), lambda i: (i, 0)),
        out_shape=jax.ShapeDtypeStruct((t, d), out_dtype),
        compiler_params=_params("parallel"),
        name="rmsnorm",
    )(x2d, g.reshape(1, d))


def _mm_body(*refs, n_pairs, has_res):
    n_in = 2 * n_pairs + int(has_res)
    o_ref = refs[n_in]
    wbf = refs[n_in + 1:]

    @pl.when(pl.program_id(1) == 0)
    def _cast():
        for p in range(n_pairs):
            wbf[p][...] = refs[2 * p + 1][...].astype(BF16)

    acc = None
    for p in range(n_pairs):
        d = jnp.dot(refs[2 * p][...], wbf[p][...], preferred_element_type=F32)
        acc = d if acc is None else acc + d
    if has_res:
        acc = acc + refs[2 * n_pairs][...]
    o_ref[...] = acc.astype(o_ref.dtype)


def _matmul(a_list, w, res, out_dtype, tm, tn, name):
    m = a_list[0].shape[0]
    n = w.shape[1]
    in_specs, args = [], []
    kp = a_list[0].shape[1]
    assert all(a.shape == (m, kp) for a in a_list) and w.shape[0] == kp * len(a_list)
    for part, a in enumerate(a_list):
        in_specs.append(pl.BlockSpec((tm, kp), lambda j, i: (i, 0)))
        in_specs.append(pl.BlockSpec((kp, tn), functools.partial(lambda j, i, r: (r, j), r=part)))
        args += [a, w]
    if res is not None:
        in_specs.append(pl.BlockSpec((tm, tn), lambda j, i: (i, j)))
        args.append(res)
    return pl.pallas_call(
        functools.partial(_mm_body, n_pairs=len(a_list), has_res=res is not None),
        grid=(n // tn, m // tm),
        in_specs=in_specs,
        out_specs=pl.BlockSpec((tm, tn), lambda j, i: (i, j)),
        out_shape=jax.ShapeDtypeStruct((m, n), out_dtype),
        scratch_shapes=[pltpu.VMEM((kp, tn), BF16) for _ in a_list],
        compiler_params=_params("arbitrary", "arbitrary"),
        name=name,
    )(*args)


def _win_body(slope_ref, sink_ref, q_ref, kp_ref, kc_ref, kn_ref, vp_ref, vc_ref, vn_ref, o_ref,
              *, nb, group, scale):
    kv = pl.program_id(1)
    n = pl.program_id(2)
    kb = jnp.concatenate([kp_ref[0], kc_ref[0], kn_ref[0]], axis=0)
    vb = jnp.concatenate([vp_ref[0], vc_ref[0], vn_ref[0]], axis=0)
    qi = lax.broadcasted_iota(jnp.int32, (BLOCK, 3 * BLOCK), 0)
    kj = lax.broadcasted_iota(jnp.int32, (BLOCK, 3 * BLOCK), 1)
    arel = jnp.abs(BLOCK + qi - kj)
    kpos = (n - 1) * BLOCK + kj
    valid = (arel <= WINDOW) & (kpos >= 0) & (kpos < nb * BLOCK)
    arel_f = arel.astype(F32)
    q = q_ref[0]
    for g in range(group):
        h = kv * group + g
        sk = sink_ref[h]
        s = lax.dot_general(q[:, g * HEAD_DIM:(g + 1) * HEAD_DIM], kb, _NT,
                            preferred_element_type=F32) * scale
        s = jnp.where(valid, s - slope_ref[h] * arel_f, NEG)
        m = jnp.maximum(jnp.max(s, axis=-1, keepdims=True), sk)
        p = jnp.exp(s - m)
        den = jnp.sum(p, axis=-1, keepdims=True) + jnp.exp(sk - m)
        o = jnp.dot(p.astype(BF16), vb, preferred_element_type=F32) / den
        o_ref[0, :, g * HEAD_DIM:(g + 1) * HEAD_DIM] = o.astype(o_ref.dtype)


def _window_attention(proj, slopes, sink, *, n_heads, n_kv, q_col, k_col, v_col):
    b, s, _ = proj.shape
    nb = s // BLOCK
    group = n_heads // n_kv
    qw = group * HEAD_DIM
    qb0, kb0, vb0 = q_col // qw, k_col // HEAD_DIM, v_col // HEAD_DIM
    assert q_col % qw == 0 and k_col % HEAD_DIM == 0 and v_col % HEAD_DIM == 0

    def kv_spec(col0, shift):
        def imap(bi, kv, n):
            return (bi, jnp.clip(n + shift, 0, nb - 1), col0 + kv)
        return pl.BlockSpec((1, BLOCK, HEAD_DIM), imap)

    smem = pl.BlockSpec(memory_space=pltpu.SMEM)
    return pl.pallas_call(
        functools.partial(_win_body, nb=nb, group=group, scale=HEAD_DIM ** -0.5),
        grid=(b, n_kv, nb),
        in_specs=[smem, smem,
                  pl.BlockSpec((1, BLOCK, qw), lambda bi, kv, n: (bi, n, qb0 + kv)),
                  kv_spec(kb0, -1), kv_spec(kb0, 0), kv_spec(kb0, 1),
                  kv_spec(vb0, -1), kv_spec(vb0, 0), kv_spec(vb0, 1)],
        out_specs=pl.BlockSpec((1, BLOCK, qw), lambda bi, kv, n: (bi, n, kv)),
        out_shape=jax.ShapeDtypeStruct((b, s, n_heads * HEAD_DIM), BF16),
        compiler_params=_params("parallel", "parallel", "parallel"),
        name="window_attn",
    )(slopes, sink, proj, proj, proj, proj, proj, proj, proj)


def _qkprep_body(x_ref, g_ref, cos_ref, sin_ref, o_ref, *, n_qchunks, heads_per_chunk, scale):
    j = pl.program_id(2)
    is_k = j >= n_qchunks
    gain = jnp.where(is_k, g_ref[1:2, :], g_ref[0:1, :])
    mult = jnp.where(is_k, 1.0, scale)
    cos = cos_ref[...]
    sin = sin_ref[...]
    lane = lax.broadcasted_iota(jnp.int32, cos.shape, 1)
    first = (lane % (HEAD_DIM // 2)) < (HEAD_DIM // 4)
    for hd in range(heads_per_chunk):
        x = x_ref[0, :, hd * HEAD_DIM:(hd + 1) * HEAD_DIM].astype(F32)
        ms = jnp.mean(x * x, axis=-1, keepdims=True)
        y = x * lax.rsqrt(ms + EPS) * gain
        partner = jnp.where(first, pltpu.roll(y, HEAD_DIM - HEAD_DIM // 4, 1),
                            pltpu.roll(y, HEAD_DIM // 4, 1))
        r = y * cos + partner * sin
        o_ref[0, :, hd * HEAD_DIM:(hd + 1) * HEAD_DIM] = (r * mult).astype(o_ref.dtype)


def _rope_tables(s):
    rows = s // GRID_W
    row = jnp.repeat(jnp.arange(rows), GRID_W).astype(F32)
    col = jnp.tile(jnp.arange(GRID_W), rows).astype(F32)
    quarter = HEAD_DIM // 4
    inv = ROPE_THETA ** (-jnp.arange(quarter, dtype=F32) / quarter)
    ang_r = row[:, None] * inv[None, :]
    ang_c = col[:, None] * inv[None, :]
    cos = jnp.concatenate([jnp.cos(ang_r)] * 2 + [jnp.cos(ang_c)] * 2, axis=-1)
    sin = jnp.concatenate([-jnp.sin(ang_r), jnp.sin(ang_r), -jnp.sin(ang_c), jnp.sin(ang_c)], axis=-1)
    return cos, sin


def _qk_prep(proj, gains, *, q_col, n_q_heads, n_k_heads, ts=256):
    b, s, _ = proj.shape
    cw = n_k_heads * HEAD_DIM
    assert q_col % cw == 0 and (n_q_heads * HEAD_DIM) % cw == 0
    n_qchunks = n_q_heads * HEAD_DIM // cw
    c0 = q_col // cw
    cos, sin = _rope_tables(s)
    return pl.pallas_call(
        functools.partial(_qkprep_body, n_qchunks=n_qchunks, heads_per_chunk=n_k_heads,
                          scale=HEAD_DIM ** -0.5),
        grid=(b, s // ts, n_qchunks + 1),
        in_specs=[pl.BlockSpec((1, ts, cw), lambda bi, i, j: (bi, i, c0 + j)),
                  pl.BlockSpec((2, HEAD_DIM), lambda bi, i, j: (0, 0)),
                  pl.BlockSpec((ts, HEAD_DIM), lambda bi, i, j: (i, 0)),
                  pl.BlockSpec((ts, HEAD_DIM), lambda bi, i, j: (i, 0))],
        out_specs=pl.BlockSpec((1, ts, cw), lambda bi, i, j: (bi, i, j)),
        out_shape=jax.ShapeDtypeStruct((b, s, (n_qchunks + 1) * cw), BF16),
        compiler_params=_params("parallel", "parallel", "parallel"),
        name="qk_prep",
    )(proj, gains, cos, sin)


def _grid_attn_body(q_ref, k_ref, v_ref, o_ref, *, tq, group, kc, s_len):
    q = q_ref[0]
    qs = jnp.concatenate([q[:, g * HEAD_DIM:(g + 1) * HEAD_DIM] for g in range(group)], axis=0)
    rows = group * tq

    def step(j, carry):
        m, l, acc = carry
        off = pl.multiple_of(j * kc, kc)
        kch = k_ref[0, pl.ds(off, kc), :]
        vch = v_ref[0, pl.ds(off, kc), :]
        s = lax.dot_general(qs, kch, _NT, preferred_element_type=F32)
        m_new = jnp.maximum(m, jnp.max(s, axis=-1, keepdims=True))
        alpha = jnp.exp(m - m_new)
        p = jnp.exp(s - m_new)
        l = alpha * l + jnp.sum(p, axis=-1, keepdims=True)
        acc = alpha * acc + jnp.dot(p.astype(BF16), vch, preferred_element_type=F32)
        return m_new, l, acc

    init = (jnp.full((rows, 1), NEG, F32), jnp.zeros((rows, 1), F32), jnp.zeros((rows, HEAD_DIM), F32))
    _, l, acc = lax.fori_loop(0, s_len // kc, step, init)
    o = acc / l
    for g in range(group):
        o_ref[0, :, g * HEAD_DIM:(g + 1) * HEAD_DIM] = o[g * tq:(g + 1) * tq].astype(o_ref.dtype)


def _grid_attention(qk, proj, *, n_heads, n_kv, v_col, tq=128, kc=512):
    b, s, _ = qk.shape
    group = n_heads // n_kv
    qw = group * HEAD_DIM
    kc = min(kc, s)
    vb0 = v_col // HEAD_DIM
    return pl.pallas_call(
        functools.partial(_grid_attn_body, tq=tq, group=group, kc=kc, s_len=s),
        grid=(b, n_kv, s // tq),
        in_specs=[pl.BlockSpec((1, tq, qw), lambda bi, kv, i: (bi, i, kv)),
                  pl.BlockSpec((1, s, HEAD_DIM), lambda bi, kv, i: (bi, 0, n_heads + kv)),
                  pl.BlockSpec((1, s, HEAD_DIM), lambda bi, kv, i: (bi, 0, vb0 + kv))],
        out_specs=pl.BlockSpec((1, tq, qw), lambda bi, kv, i: (bi, i, kv)),
        out_shape=jax.ShapeDtypeStruct((b, s, n_heads * HEAD_DIM), BF16),
        compiler_params=_params("parallel", "parallel", "arbitrary"),
        name="grid_attn",
    )(qk, qk, proj)


def _diff_attn_body(slope_ref, lamc_ref, sub_ref, q_ref, k_ref, v_ref, o_ref,
                    *, tq, kc, s_len, scale, lambda_init):
    h = pl.program_id(1)
    i = pl.program_id(2)
    slope = slope_ref[h]
    q = q_ref[0]
    q1, q2 = q[:, :DIFF_DIM], q[:, DIFF_DIM:]
    qpos = i * tq + lax.broadcasted_iota(jnp.int32, (tq, kc), 0)
    kj = lax.broadcasted_iota(jnp.int32, (tq, kc), 1)

    def softmax_step(qc, kch, vch, bias, m, l, acc):
        s = lax.dot_general(qc, kch, _NT, preferred_element_type=F32) * scale + bias
        m_new = jnp.maximum(m, jnp.max(s, axis=-1, keepdims=True))
        alpha = jnp.exp(m - m_new)
        p = jnp.exp(s - m_new)
        l = alpha * l + jnp.sum(p, axis=-1, keepdims=True)
        acc = alpha * acc + jnp.dot(p.astype(BF16), vch, preferred_element_type=F32)
        return m_new, l, acc

    def step(j, carry):
        m1, l1, a1, m2, l2, a2 = carry
        off = pl.multiple_of(j * kc, kc)
        kch = k_ref[0, pl.ds(off, kc), :]
        vch = v_ref[0, pl.ds(off, kc), :]
        bias = -slope * jnp.abs(qpos - (off + kj)).astype(F32)
        m1, l1, a1 = softmax_step(q1, kch[:, :DIFF_DIM], vch, bias, m1, l1, a1)
        m2, l2, a2 = softmax_step(q2, kch[:, DIFF_DIM:], vch, bias, m2, l2, a2)
        return m1, l1, a1, m2, l2, a2

    def init():
        return (jnp.full((tq, 1), NEG, F32), jnp.zeros((tq, 1), F32), jnp.zeros((tq, 2 * DIFF_DIM), F32))

    _, l1, a1, _, l2, a2 = lax.fori_loop(0, s_len // kc, step, init() + init())
    lf = lamc_ref[...]
    lam = (jnp.exp(jnp.sum(lf[0:1] * lf[1:2], axis=-1, keepdims=True))
           - jnp.exp(jnp.sum(lf[2:3] * lf[3:4], axis=-1, keepdims=True)) + lambda_init)
    o = a1 / l1 - lam * (a2 / l2)
    ms = jnp.mean(o * o, axis=-1, keepdims=True)
    o = o * lax.rsqrt(ms + EPS) * sub_ref[...] * (1.0 - lambda_init)
    o_ref[0] = o.astype(o_ref.dtype)


def _diff_attention(proj, slopes, c_lambda, c_subnorm, lambda_init, *, n_heads, tq=256, kc=512):
    b, s, _ = proj.shape
    hw = 2 * DIFF_DIM
    kc = min(kc, s)
    tq = min(tq, s)
    return pl.pallas_call(
        functools.partial(_diff_attn_body, tq=tq, kc=kc, s_len=s, scale=DIFF_DIM ** -0.5,
                          lambda_init=lambda_init),
        grid=(b, n_heads, s // tq),
        in_specs=[pl.BlockSpec(memory_space=pltpu.SMEM),
                  pl.BlockSpec((4, DIFF_DIM), lambda bi, h, i: (0, 0)),
                  pl.BlockSpec((1, hw), lambda bi, h, i: (0, 0)),
                  pl.BlockSpec((1, tq, hw), lambda bi, h, i: (bi, i, h)),
                  pl.BlockSpec((1, s, hw), lambda bi, h, i: (bi, 0, n_heads + h)),
                  pl.BlockSpec((1, s, hw), lambda bi, h, i: (bi, 0, 2 * n_heads + h))],
        out_specs=pl.BlockSpec((1, tq, hw), lambda bi, h, i: (bi, i, h)),
        out_shape=jax.ShapeDtypeStruct((b, s, n_heads * hw), BF16),
        compiler_params=_params("parallel", "parallel", "arbitrary"),
        name="diff_attn",
    )(slopes, c_lambda, c_subnorm.reshape(1, hw), proj, proj, proj)


def _router_body(h_ref, g_ref, wrt_ref, aff_ref):
    x = h_ref[...]
    ms = jnp.mean(x * x, axis=-1, keepdims=True)
    xn = x * lax.rsqrt(ms + EPS) * g_ref[...]
    logits = lax.dot_general(wrt_ref[...], xn, _NT, precision=lax.Precision.HIGHEST,
                             preferred_element_type=F32)
    m = jnp.max(logits, axis=0, keepdims=True)
    e = jnp.exp(logits - m)
    aff_ref[0] = e / jnp.sum(e, axis=0, keepdims=True)


def _router(h2d, g, w_router, b, s, tm=512):
    t, d = h2d.shape
    tm = min(tm, s)
    spb = s // tm
    ne = w_router.shape[1]
    return pl.pallas_call(
        _router_body,
        grid=(t // tm,),
        in_specs=[pl.BlockSpec((tm, d), lambda i: (i, 0)),
                  pl.BlockSpec((1, d), lambda i: (0, 0)),
                  pl.BlockSpec((ne, d), lambda i: (0, 0))],
        out_specs=pl.BlockSpec((1, ne, tm), lambda i: (i // spb, 0, i % spb)),
        out_shape=jax.ShapeDtypeStruct((b, ne, s), F32),
        compiler_params=_params("parallel"),
        name="router",
    )(h2d, g.reshape(1, d), w_router.T)


def _cumsum_lanes(mask, tri):
    rows, n = mask.shape
    carry = jnp.zeros((rows, 1), F32)
    out = []
    for blk in range(n // 128):
        c = jnp.dot(mask[:, blk * 128:(blk + 1) * 128].astype(BF16), tri, preferred_element_type=F32) + carry
        out.append(c)
        carry = c[:, 127:128]
    return jnp.concatenate(out, axis=1)


def _select_body(aff_ref, pos_ref, *, cap):
    a = aff_ref[0]
    bits = pltpu.bitcast(a, jnp.int32)
    ne = a.shape[0]
    thr = jnp.zeros((ne, 1), jnp.int32)
    for bit in range(30, -1, -1):
        cand = thr | (1 << bit)
        cnt = jnp.sum((bits >= cand).astype(jnp.int32), axis=1, keepdims=True)
        thr = jnp.where(cnt >= cap, cand, thr)
    gt = bits > thr
    eq = bits == thr
    need = (cap - jnp.sum(gt.astype(jnp.int32), axis=1, keepdims=True)).astype(F32)
    r = lax.broadcasted_iota(jnp.int32, (128, 128), 0)
    c = lax.broadcasted_iota(jnp.int32, (128, 128), 1)
    tri = (r <= c).astype(BF16)
    sel = gt | (eq & (_cumsum_lanes(eq.astype(F32), tri) <= need))
    pos = _cumsum_lanes(sel.astype(F32), tri) - 1.0
    pos_ref[0] = jnp.where(sel, pos, -1.0)


def _select(aff, cap):
    b, ne, s = aff.shape
    return pl.pallas_call(
        functools.partial(_select_body, cap=cap),
        grid=(b,),
        in_specs=[pl.BlockSpec((1, ne, s), lambda bi: (bi, 0, 0))],
        out_specs=pl.BlockSpec((1, ne, s), lambda bi: (bi, 0, 0)),
        out_shape=jax.ShapeDtypeStruct((b, ne, s), F32),
        compiler_params=_params("parallel"),
        name="select",
    )(aff)


def _compact_body(pos_ref, aff_ref, idx_ref, gate_ref, *, tc):
    c0 = pl.program_id(1) * tc
    pos = pos_ref[0]
    s = pos.shape[1]
    slot = (c0 + lax.broadcasted_iota(jnp.int32, (tc, s), 0)).astype(F32)
    tok = lax.broadcasted_iota(jnp.int32, (tc, s), 1)
    hit = pos == slot
    idx_ref[0] = jnp.sum(jnp.where(hit, tok, 0), axis=1, keepdims=True)
    gate_ref[0] = jnp.sum(jnp.where(hit, aff_ref[0], 0.0), axis=1, keepdims=True)


def _compact(pos, aff, cap, tc=128):
    b, ne, s = pos.shape
    tc = min(tc, cap)
    row = pl.BlockSpec((1, 1, s), lambda r, j: (r, 0, 0))
    col = pl.BlockSpec((1, tc, 1), lambda r, j: (r, j, 0))
    return pl.pallas_call(
        functools.partial(_compact_body, tc=tc),
        grid=(b * ne, cap // tc),
        in_specs=[row, row],
        out_specs=[col, col],
        out_shape=[jax.ShapeDtypeStruct((b * ne, cap, 1), jnp.int32),
                   jax.ShapeDtypeStruct((b * ne, cap, 1), F32)],
        compiler_params=_params("parallel", "parallel"),
        name="compact",
    )(pos.reshape(b * ne, 1, s), aff.reshape(b * ne, 1, s))


def _ffn_body(idx_ref, h_hbm, g_ref, gate_ref, wg_ref, wu_ref, wd_ref, o_ref, land, xg, hid, sems,
              *, cap, s_len, n_b, n_f, piece):
    e = pl.program_id(0)
    step = pl.program_id(1)
    ne = pl.num_programs(0)
    rows = n_b * cap
    n_pieces = rows // piece

    def row_copy(src_row, slot, r):
        return pltpu.make_async_copy(h_hbm.at[pl.ds(src_row, 1)], land.at[slot, pl.ds(r, 1)], sems.at[slot])

    def issue(p):
        bi, c0 = divmod(p * piece, cap)
        base = (bi * ne + e) * cap + c0

        def body(r, carry):
            row_copy(bi * s_len + idx_ref[base + r], p % 2, r).start()
            return carry

        lax.fori_loop(0, piece, body, 0)

    def drain(p):
        def body(r, carry):
            row_copy(0, p % 2, r).wait()
            return carry

        lax.fori_loop(0, piece, body, 0)

    @pl.when(step == 0)
    def _gather_and_norm():
        issue(0)
        for p in range(n_pieces):
            if p + 1 < n_pieces:
                issue(p + 1)
            drain(p)
            x = land[p % 2]
            ms = jnp.mean(x * x, axis=-1, keepdims=True)
            xg[p * piece:(p + 1) * piece, :] = (x * lax.rsqrt(ms + EPS) * g_ref[...]).astype(BF16)

    @pl.when(step < n_f)
    def _up():
        x = xg[...]
        gt = jnp.dot(x, wg_ref[0, 0].astype(BF16), preferred_element_type=F32)
        up = jnp.dot(x, wu_ref[0, 0].astype(BF16), preferred_element_type=F32)
        hid[step] = (gt * jax.nn.sigmoid(gt) * up).astype(BF16)

    @pl.when(step >= n_f)
    def _down():
        hcat = jnp.concatenate([hid[f] for f in range(n_f)], axis=1)
        y = jnp.dot(hcat, wd_ref[0, 0].astype(BF16), preferred_element_type=F32)
        o_ref[0] = y * gate_ref[:, 0].reshape(rows, 1)


def _expert_ffn(idx_flat, h2d, g, gates, w_gate, w_up, w_down, layer, b, s, cap, tf=256, tn=512, piece=128):
    t, d = h2d.shape
    _, ne, _, dff = w_gate.shape
    tf = min(tf, dff)
    tn = min(tn, d)
    n_f = dff // tf
    n_d = d // tn
    rows = b * cap
    piece = min(piece, cap)
    grid_spec = pltpu.PrefetchScalarGridSpec(
        num_scalar_prefetch=1,
        grid=(ne, n_f + n_d),
        in_specs=[pl.BlockSpec(memory_space=pl.ANY),
                  pl.BlockSpec((1, d), lambda e, st, idx: (0, 0)),
                  pl.BlockSpec((b, 1, cap, 1), lambda e, st, idx: (0, e, 0, 0)),
                  pl.BlockSpec((1, 1, d, tf), lambda e, st, idx: (layer, e, 0, jnp.minimum(st, n_f - 1))),
                  pl.BlockSpec((1, 1, d, tf), lambda e, st, idx: (layer, e, 0, jnp.minimum(st, n_f - 1))),
                  pl.BlockSpec((1, 1, dff, tn), lambda e, st, idx: (layer, e, 0, jnp.maximum(st - n_f, 0)))],
        out_specs=pl.BlockSpec((1, rows, tn), lambda e, st, idx: (e, 0, jnp.maximum(st - n_f, 0))),
        scratch_shapes=[pltpu.VMEM((2, piece, d), F32), pltpu.VMEM((rows, d), BF16),
                        pltpu.VMEM((n_f, rows, tf), BF16), pltpu.SemaphoreType.DMA((2,))],
    )
    return pl.pallas_call(
        functools.partial(_ffn_body, cap=cap, s_len=s, n_b=b, n_f=n_f, piece=piece),
        grid_spec=grid_spec,
        out_shape=jax.ShapeDtypeStruct((ne, rows, d), F32),
        compiler_params=_params("arbitrary", "arbitrary"),
        name="expert_ffn",
    )(idx_flat, h2d, g.reshape(1, d), gates.reshape(b, ne, cap, 1), w_gate, w_up, w_down)


def _combine_body(idx_ref, h_hbm, y_ref, o_ref, sem, *, cap, ne, rows_per_group, tn):
    b = pl.program_id(0)
    j = pl.program_id(1)
    e = pl.program_id(2)

    @pl.when(e == 0)
    def _init():
        cp = pltpu.make_async_copy(h_hbm.at[b, :, pl.ds(pl.multiple_of(j * tn, tn), tn)], o_ref.at[0], sem)
        cp.start()
        cp.wait()

    base = (b * ne + e) * cap

    def group(gi, carry):
        c0 = pl.multiple_of(gi * rows_per_group, rows_per_group)
        toks = [idx_ref[base + c0 + r] for r in range(rows_per_group)]
        cur = [o_ref[0, pl.ds(toks[r], 1), :] for r in range(rows_per_group)]
        upd = [cur[r] + y_ref[0, pl.ds(c0 + r, 1), :] for r in range(rows_per_group)]
        for r in range(rows_per_group):
            o_ref[0, pl.ds(toks[r], 1), :] = upd[r]
        return carry

    lax.fori_loop(0, cap // rows_per_group, group, 0)


def _combine(idx_flat, h, y, cap, tn=1024, rows_per_group=4):
    b, s, d = h.shape
    ne = y.shape[0]
    tn = min(tn, d)
    rows_per_group = min(rows_per_group, cap)
    grid_spec = pltpu.PrefetchScalarGridSpec(
        num_scalar_prefetch=1,
        grid=(b, d // tn, ne),
        in_specs=[pl.BlockSpec(memory_space=pl.ANY),
                  pl.BlockSpec((1, cap, tn), lambda bi, j, e, idx: (e, bi, j))],
        out_specs=pl.BlockSpec((1, s, tn), lambda bi, j, e, idx: (bi, 0, j)),
        scratch_shapes=[pltpu.SemaphoreType.DMA(())],
    )
    return pl.pallas_call(
        functools.partial(_combine_body, cap=cap, ne=ne, rows_per_group=rows_per_group, tn=tn),
        grid_spec=grid_spec,
        out_shape=jax.ShapeDtypeStruct((b, s, d), F32),
        compiler_params=_params("arbitrary", "arbitrary", "arbitrary"),
        name="combine",
    )(idx_flat, h, y)


def _moe(h, g, w_router, w_gate, w_up, w_down, layer):
    b, s, d = h.shape
    ne = w_router.shape[1]
    cap = CAPACITY_FACTOR * s // ne
    h2d = h.reshape(b * s, d)
    aff = _router(h2d, g, w_router, b, s)
    pos = _select(aff, cap)
    idx, gates = _compact(pos, aff, cap)
    idx_flat = idx.reshape(b * ne * cap)
    y = _expert_ffn(idx_flat, h2d, g, gates, w_gate, w_up, w_down, layer, b, s, cap)
    return _combine(idx_flat, h, y, cap)


def _alibi_slopes(n):
    return 2.0 ** (-8.0 * jnp.arange(1, n + 1, dtype=F32) / n)


def _mm_tiles(m, n):
    tm = next(t for t in (512, 256, 128, 8) if m % t == 0)
    tn = next(t for t in (512, 256, 128) if n % t == 0)
    return tm, tn


def _even_mixer(h, g, w_in, w_out, sink, qnorm, knorm):
    b, s, d = h.shape
    t = b * s
    n_heads = d // HEAD_DIM
    ha, hb = n_heads // 2, n_heads // 2
    kva, kvb = ha // 4, hb // 4
    q_a, kv_a, q_b, kv_b = ha * HEAD_DIM, kva * HEAD_DIM, hb * HEAD_DIM, kvb * HEAD_DIM
    h2d = h.reshape(t, d)
    hn = _rmsnorm(h2d, g, BF16)
    w_total = w_in.shape[1]
    tm, tn = _mm_tiles(t, w_total)
    proj = _matmul([hn], w_in, None, BF16, tm, tn, "in_proj_even").reshape(b, s, w_total)
    out_a = _window_attention(proj, _alibi_slopes(ha), sink, n_heads=ha, n_kv=kva,
                              q_col=0, k_col=q_a, v_col=q_a + kv_a)
    qb0 = q_a + 2 * kv_a
    qk = _qk_prep(proj, jnp.stack([qnorm, knorm]), q_col=qb0, n_q_heads=hb, n_k_heads=kvb)
    out_b = _grid_attention(qk, proj, n_heads=hb, n_kv=kvb, v_col=qb0 + q_b + kv_b)
    tm, tn = _mm_tiles(t, d)
    out = _matmul([out_a.reshape(t, q_a), out_b.reshape(t, q_b)], w_out, h2d, F32, tm, tn, "out_proj_even")
    return out.reshape(b, s, d)


def _odd_mixer(h, g, w_in, w_out, c_lambda, c_subnorm, lambda_init):
    b, s, d = h.shape
    t = b * s
    n_heads = d // (2 * DIFF_DIM)
    h2d = h.reshape(t, d)
    hn = _rmsnorm(h2d, g, BF16)
    w_total = w_in.shape[1]
    tm, tn = _mm_tiles(t, w_total)
    proj = _matmul([hn], w_in, None, BF16, tm, tn, "in_proj_odd").reshape(b, s, w_total)
    mix = _diff_attention(proj, _alibi_slopes(n_heads), c_lambda, c_subnorm, lambda_init, n_heads=n_heads)
    tm, tn = _mm_tiles(t, d)
    out = _matmul([mix.reshape(t, d)], w_out, h2d, F32, tm, tn, "out_proj_odd")
    return out.reshape(b, s, d)


def kernel(x, norm_mix, norm_ffn, norm_final, w_in_even, w_out_even, sink_a, qnorm_b, knorm_b,
           w_in_odd, w_out_odd, c_lambda, c_subnorm, w_router, w_gate, w_up, w_down):
    b, s, d = x.shape
    depth = norm_mix.shape[0]
    h = x
    for layer in range(depth):
        i = layer // 2
        if layer % 2 == 0:
            h = _even_mixer(h, norm_mix[layer], w_in_even[i], w_out_even[i], sink_a[i], qnorm_b[i], knorm_b[i])
        else:
            lambda_init = 0.8 - 0.6 * math.exp(-0.3 * layer)
            h = _odd_mixer(h, norm_mix[layer], w_in_odd[i], w_out_odd[i], c_lambda[i], c_subnorm[i], lambda_init)
        h = _moe(h, norm_ffn[layer], w_router[layer], w_gate, w_up, w_down, layer)
    return _rmsnorm(h.reshape(b * s, d), norm_final, F32).reshape(b, s, d)
```

```python
import functools
import math

import jax
import jax.numpy as jnp
from jax import lax
from jax.experimental import pallas as pl
from jax.experimental.pallas import tpu as pltpu

F32 = jnp.float32
BF16 = jnp.bfloat16

HEAD_DIM = 128
WINDOW = 128
BLOCK = 128
GRID_W = 64
ROPE_THETA = 10000.0
DIFF_DIM = 128
N_EXPERTS = 16
CAPACITY_FACTOR = 2
EPS = 1e-6
NEG = -1e30
LOG2E = math.log2(math.e)

VMEM_LIMIT_BYTES = 56 * 1024 * 1024

_NT = (((1,), (1,)), ((), ()))


def _params(*sem):
    return pltpu.CompilerParams(dimension_semantics=sem, vmem_limit_bytes=VMEM_LIMIT_BYTES)


def _rmsnorm_body(x_ref, g_ref, o_ref):
    x = x_ref[...]
    ms = jnp.mean(x * x, axis=-1, keepdims=True)
    o_ref[...] = (x * lax.rsqrt(ms + EPS) * g_ref[...]).astype(o_ref.dtype)


def _rmsnorm(x2d, g, out_dtype, tm=256):
    t, d = x2d.shape
    return pl.pallas_call(
        _rmsnorm_body,
        grid=(t // tm,),
        in_specs=[pl.BlockSpec((tm, d), lambda i: (i, 0)),
                  pl.BlockSpec((1, d), lambda i: (0, 0))],
        out_specs=pl.BlockSpec((tm, d), lambda i: (i, 0)),
        out_shape=jax.ShapeDtypeStruct((t, d), out_dtype),
        compiler_params=_params("parallel"),
        name="rmsnorm",
    )(x2d, g.reshape(1, d))


def _mm_body(*refs, n_pairs, has_res, has_scale):
    n_in = 2 * n_pairs + int(has_res) + int(has_scale)
    o_ref = refs[n_in]
    wbf = refs[n_in + 1:]

    @pl.when(pl.program_id(1) == 0)
    def _cast():
        for p in range(n_pairs):
            wbf[p][...] = refs[2 * p + 1][...].astype(BF16)

    acc = None
    for p in range(n_pairs):
        d = jnp.dot(refs[2 * p][...], wbf[p][...], preferred_element_type=F32)
        acc = d if acc is None else acc + d
    if has_res:
        acc = acc + refs[2 * n_pairs][...]
    if has_scale:
        acc = acc * refs[n_in - 1][...]
    o_ref[...] = acc.astype(o_ref.dtype)


def _matmul(a_list, w, res, out_dtype, tm, tn, name, col_scale=None):
    m = a_list[0].shape[0]
    n = w.shape[1]
    in_specs, args = [], []
    kp = a_list[0].shape[1]
    assert all(a.shape == (m, kp) for a in a_list) and w.shape[0] == kp * len(a_list)
    for part, a in enumerate(a_list):
        in_specs.append(pl.BlockSpec((tm, kp), lambda j, i: (i, 0)))
        in_specs.append(pl.BlockSpec((kp, tn), functools.partial(lambda j, i, r: (r, j), r=part)))
        args += [a, w]
    if res is not None:
        in_specs.append(pl.BlockSpec((tm, tn), lambda j, i: (i, j)))
        args.append(res)
    if col_scale is not None:
        in_specs.append(pl.BlockSpec((1, tn), lambda j, i: (0, j)))
        args.append(col_scale.reshape(1, n))
    return pl.pallas_call(
        functools.partial(_mm_body, n_pairs=len(a_list), has_res=res is not None,
                          has_scale=col_scale is not None),
        grid=(n // tn, m // tm),
        in_specs=in_specs,
        out_specs=pl.BlockSpec((tm, tn), lambda j, i: (i, j)),
        out_shape=jax.ShapeDtypeStruct((m, n), out_dtype),
        scratch_shapes=[pltpu.VMEM((kp, tn), BF16) for _ in a_list],
        compiler_params=_params("arbitrary", "arbitrary"),
        name=name,
    )(*args)


def _win_body(slope_ref, sink_ref, q_ref, kp_ref, kc_ref, kn_ref, vp_ref, vc_ref, vn_ref, o_ref,
              *, nb, group, scale):
    kv = pl.program_id(1)
    n = pl.program_id(2)
    kb = jnp.concatenate([kp_ref[0], kc_ref[0], kn_ref[0]], axis=0)
    vb = jnp.concatenate([vp_ref[0], vc_ref[0], vn_ref[0]], axis=0)
    qi = lax.broadcasted_iota(jnp.int32, (BLOCK, 3 * BLOCK), 0)
    kj = lax.broadcasted_iota(jnp.int32, (BLOCK, 3 * BLOCK), 1)
    arel = jnp.abs(BLOCK + qi - kj)
    kpos = (n - 1) * BLOCK + kj
    valid = (arel <= WINDOW) & (kpos >= 0) & (kpos < nb * BLOCK)
    arel_f = arel.astype(F32)
    q = q_ref[0]
    for g in range(group):
        h = kv * group + g
        sk = sink_ref[h]
        s = lax.dot_general(q[:, g * HEAD_DIM:(g + 1) * HEAD_DIM], kb, _NT,
                            preferred_element_type=F32) * scale
        s = jnp.where(valid, s - slope_ref[h] * arel_f, NEG)
        m = jnp.maximum(jnp.max(s, axis=-1, keepdims=True), sk)
        p = jnp.exp(s - m)
        den = jnp.sum(p, axis=-1, keepdims=True) + jnp.exp(sk - m)
        o = jnp.dot(p.astype(BF16), vb, preferred_element_type=F32) / den
        o_ref[0, :, g * HEAD_DIM:(g + 1) * HEAD_DIM] = o.astype(o_ref.dtype)


def _window_attention(proj, slopes, sink, *, n_heads, n_kv, q_col, k_col, v_col):
    b, s, _ = proj.shape
    nb = s // BLOCK
    group = n_heads // n_kv
    qw = group * HEAD_DIM
    qb0, kb0, vb0 = q_col // qw, k_col // HEAD_DIM, v_col // HEAD_DIM
    assert q_col % qw == 0 and k_col % HEAD_DIM == 0 and v_col % HEAD_DIM == 0

    def kv_spec(col0, shift):
        def imap(bi, kv, n):
            return (bi, jnp.clip(n + shift, 0, nb - 1), col0 + kv)
        return pl.BlockSpec((1, BLOCK, HEAD_DIM), imap)

    smem = pl.BlockSpec(memory_space=pltpu.SMEM)
    return pl.pallas_call(
        functools.partial(_win_body, nb=nb, group=group, scale=HEAD_DIM ** -0.5),
        grid=(b, n_kv, nb),
        in_specs=[smem, smem,
                  pl.BlockSpec((1, BLOCK, qw), lambda bi, kv, n: (bi, n, qb0 + kv)),
                  kv_spec(kb0, -1), kv_spec(kb0, 0), kv_spec(kb0, 1),
                  kv_spec(vb0, -1), kv_spec(vb0, 0), kv_spec(vb0, 1)],
        out_specs=pl.BlockSpec((1, BLOCK, qw), lambda bi, kv, n: (bi, n, kv)),
        out_shape=jax.ShapeDtypeStruct((b, s, n_heads * HEAD_DIM), BF16),
        compiler_params=_params("parallel", "parallel", "parallel"),
        name="window_attn",
    )(slopes, sink, proj, proj, proj, proj, proj, proj, proj)


def _qkprep_body(x_ref, g_ref, cos_ref, sin_ref, o_ref, *, n_qchunks, heads_per_chunk, scale):
    j = pl.program_id(2)
    is_k = j >= n_qchunks
    gain = jnp.where(is_k, g_ref[1:2, :], g_ref[0:1, :])
    mult = jnp.where(is_k, 1.0, scale)
    cos = cos_ref[...]
    sin = sin_ref[...]
    lane = lax.broadcasted_iota(jnp.int32, cos.shape, 1)
    first = (lane % (HEAD_DIM // 2)) < (HEAD_DIM // 4)
    for hd in range(heads_per_chunk):
        x = x_ref[0, :, hd * HEAD_DIM:(hd + 1) * HEAD_DIM].astype(F32)
        ms = jnp.mean(x * x, axis=-1, keepdims=True)
        y = x * lax.rsqrt(ms + EPS) * gain
        partner = jnp.where(first, pltpu.roll(y, HEAD_DIM - HEAD_DIM // 4, 1),
                            pltpu.roll(y, HEAD_DIM // 4, 1))
        r = y * cos + partner * sin
        o_ref[0, :, hd * HEAD_DIM:(hd + 1) * HEAD_DIM] = (r * mult).astype(o_ref.dtype)


def _rope_tables(s):
    rows = s // GRID_W
    row = jnp.repeat(jnp.arange(rows), GRID_W).astype(F32)
    col = jnp.tile(jnp.arange(GRID_W), rows).astype(F32)
    quarter = HEAD_DIM // 4
    inv = ROPE_THETA ** (-jnp.arange(quarter, dtype=F32) / quarter)
    ang_r = row[:, None] * inv[None, :]
    ang_c = col[:, None] * inv[None, :]
    cos = jnp.concatenate([jnp.cos(ang_r)] * 2 + [jnp.cos(ang_c)] * 2, axis=-1)
    sin = jnp.concatenate([-jnp.sin(ang_r), jnp.sin(ang_r), -jnp.sin(ang_c), jnp.sin(ang_c)], axis=-1)
    return cos, sin


def _qk_prep(proj, gains, *, q_col, n_q_heads, n_k_heads, ts=256):
    b, s, _ = proj.shape
    cw = n_k_heads * HEAD_DIM
    assert q_col % cw == 0 and (n_q_heads * HEAD_DIM) % cw == 0
    n_qchunks = n_q_heads * HEAD_DIM // cw
    c0 = q_col // cw
    cos, sin = _rope_tables(s)
    return pl.pallas_call(
        functools.partial(_qkprep_body, n_qchunks=n_qchunks, heads_per_chunk=n_k_heads,
                          scale=HEAD_DIM ** -0.5 * LOG2E),
        grid=(b, s // ts, n_qchunks + 1),
        in_specs=[pl.BlockSpec((1, ts, cw), lambda bi, i, j: (bi, i, c0 + j)),
                  pl.BlockSpec((2, HEAD_DIM), lambda bi, i, j: (0, 0)),
                  pl.BlockSpec((ts, HEAD_DIM), lambda bi, i, j: (i, 0)),
                  pl.BlockSpec((ts, HEAD_DIM), lambda bi, i, j: (i, 0))],
        out_specs=pl.BlockSpec((1, ts, cw), lambda bi, i, j: (bi, i, j)),
        out_shape=jax.ShapeDtypeStruct((b, s, (n_qchunks + 1) * cw), BF16),
        compiler_params=_params("parallel", "parallel", "parallel"),
        name="qk_prep",
    )(proj, gains, cos, sin)


def _lane_groups(x):
    return [x[:, g * 128:(g + 1) * 128] for g in range(x.shape[1] // 128)]


def _grid_attn_body(q_ref, k_ref, v_ref, o_ref, *, tq, group, kc, s_len, unroll):
    q = q_ref[0]
    qs = jnp.concatenate([q[:, g * HEAD_DIM:(g + 1) * HEAD_DIM] for g in range(group)], axis=0)
    rows = group * tq

    def step(j, carry):
        m, lp, acc = carry
        off = pl.multiple_of(j * kc, kc)
        sg = _lane_groups(lax.dot_general(qs, k_ref[0, pl.ds(off, kc), :], _NT, preferred_element_type=F32))
        cmax = jnp.max(functools.reduce(jnp.maximum, sg), axis=-1, keepdims=True)
        m_new = jnp.maximum(m, jnp.broadcast_to(cmax, (rows, 128)))
        alpha = jnp.exp2(m - m_new)
        ps = [jnp.exp2(x - m_new) for x in sg]
        lp = alpha * lp + functools.reduce(jnp.add, ps)
        p = jnp.concatenate([pg.astype(BF16) for pg in ps], axis=1)
        acc = alpha * acc + jnp.dot(p, v_ref[0, pl.ds(off, kc), :], preferred_element_type=F32)
        return m_new, lp, acc

    assert HEAD_DIM == 128
    init = (jnp.full((rows, 128), NEG, F32), jnp.zeros((rows, 128), F32), jnp.zeros((rows, HEAD_DIM), F32))
    _, lp, acc = lax.fori_loop(0, s_len // kc, step, init, unroll=unroll)
    o = acc / jnp.sum(lp, axis=-1, keepdims=True)
    for g in range(group):
        o_ref[0, :, g * HEAD_DIM:(g + 1) * HEAD_DIM] = o[g * tq:(g + 1) * tq].astype(o_ref.dtype)


def _grid_attention(qk, proj, *, n_heads, n_kv, v_col, tq=256, kc=512, unroll=2):
    b, s, _ = qk.shape
    group = n_heads // n_kv
    qw = group * HEAD_DIM
    kc = min(kc, s)
    vb0 = v_col // HEAD_DIM
    return pl.pallas_call(
        functools.partial(_grid_attn_body, tq=tq, group=group, kc=kc, s_len=s, unroll=unroll),
        grid=(b, n_kv, s // tq),
        in_specs=[pl.BlockSpec((1, tq, qw), lambda bi, kv, i: (bi, i, kv)),
                  pl.BlockSpec((1, s, HEAD_DIM), lambda bi, kv, i: (bi, 0, n_heads + kv)),
                  pl.BlockSpec((1, s, HEAD_DIM), lambda bi, kv, i: (bi, 0, vb0 + kv))],
        out_specs=pl.BlockSpec((1, tq, qw), lambda bi, kv, i: (bi, i, kv)),
        out_shape=jax.ShapeDtypeStruct((b, s, n_heads * HEAD_DIM), BF16),
        compiler_params=_params("parallel", "parallel", "arbitrary"),
        name="grid_attn",
    )(qk, qk, proj)


def _diff_attn_body(slope_ref, lamc_ref, sub_ref, q_ref, k_ref, v_ref, o_ref,
                    *, tq, kc, s_len, lambda_init, unroll):
    h = pl.program_id(1)
    r = pl.program_id(2)
    slope2 = slope_ref[h] * LOG2E
    q = q_ref[0]
    zq = jnp.zeros((tq, DIFF_DIM), q.dtype)
    qd = jnp.concatenate([jnp.concatenate([q[:, :DIFF_DIM], zq], axis=1),
                          jnp.concatenate([zq, q[:, DIFF_DIM:]], axis=1)], axis=0)
    rows = 2 * tq
    n_chunks = s_len // kc
    i0 = r * tq
    c_diag = i0 // kc
    kj = lax.broadcasted_iota(jnp.int32, (1, kc), 1).astype(F32)
    qi = (lax.broadcasted_iota(jnp.int32, (rows, 128), 0) % tq).astype(F32)

    def off_diag(c):
        return c + (c >= c_diag).astype(jnp.int32)

    def terms(cc):
        is_left = cc < c_diag
        gap = jnp.where(is_left, i0 - (cc + 1) * kc, cc * kc - i0 - tq).astype(F32)
        colterm = -slope2 * jnp.where(is_left, kc - kj, kj)
        rowterm = -slope2 * (jnp.where(is_left, qi, tq - qi) + gap)
        return colterm, rowterm

    def diag_bias():
        in_chunk = (i0 - c_diag * kc).astype(F32)
        return -slope2 * jnp.abs(qi[:, :1] + in_chunk - kj)

    def probs(cc, bias, rowterm, m, lp):
        off = pl.multiple_of(cc * kc, kc)
        ug = _lane_groups(lax.dot_general(qd, k_ref[0, pl.ds(off, kc), :], _NT,
                                          preferred_element_type=F32) + bias)
        cmax = jnp.max(functools.reduce(jnp.maximum, ug), axis=-1, keepdims=True)
        m_new = jnp.maximum(m, jnp.broadcast_to(cmax, (rows, 128)) + rowterm)
        alpha = jnp.exp2(m - m_new)
        shift = m_new - rowterm
        ps = [jnp.exp2(x - shift) for x in ug]
        lp = alpha * lp + functools.reduce(jnp.add, ps)
        p = jnp.concatenate([pg.astype(BF16) for pg in ps], axis=1)
        return m_new, lp, alpha, p

    def weighted_values(cc, alpha, p, acc):
        off = pl.multiple_of(cc * kc, kc)
        return (jnp.concatenate([alpha * ag for ag in _lane_groups(acc)], axis=1)
                + jnp.dot(p, v_ref[0, pl.ds(off, kc), :], preferred_element_type=F32))

    def step(c, carry):
        m, lp, acc, alpha, p, prev = carry
        cc = off_diag(c)
        colterm, rowterm = terms(cc)
        m, lp, alpha_next, p_next = probs(cc, colterm, rowterm, m, lp)
        acc = weighted_values(prev, alpha, p, acc)
        return m, lp, acc, alpha_next, p_next, cc

    m0 = jnp.full((rows, 128), NEG, F32)
    m, lp, alpha, p = probs(c_diag, diag_bias(), 0.0, m0, jnp.zeros((rows, 128), F32))
    carry = (m, lp, jnp.zeros((rows, 2 * DIFF_DIM), F32), alpha, p, c_diag)
    _, lp, acc, alpha, p, last = lax.fori_loop(0, n_chunks - 1, step, carry, unroll=unroll)
    acc = weighted_values(last, alpha, p, acc)
    l = jnp.sum(lp, axis=-1, keepdims=True)
    a1, a2, l1, l2 = acc[:tq], acc[tq:], l[:tq], l[tq:]
    lf = lamc_ref[...]
    lam = (jnp.exp(jnp.sum(lf[0:1] * lf[1:2], axis=-1, keepdims=True))
           - jnp.exp(jnp.sum(lf[2:3] * lf[3:4], axis=-1, keepdims=True)) + lambda_init)
    o = a1 / l1 - lam * (a2 / l2)
    ms = jnp.mean(o * o, axis=-1, keepdims=True)
    o = o * lax.rsqrt(ms + EPS) * sub_ref[...] * (1.0 - lambda_init)
    o_ref[0] = o.astype(o_ref.dtype)


def _diff_attention(proj, slopes, c_lambda, c_subnorm, lambda_init, *, n_heads, tq=256, kc=512, unroll=2):
    b, s, _ = proj.shape
    hw = 2 * DIFF_DIM
    kc = min(kc, s)
    tq = min(tq, kc)
    assert kc % tq == 0 and s % kc == 0
    return pl.pallas_call(
        functools.partial(_diff_attn_body, tq=tq, kc=kc, s_len=s, lambda_init=lambda_init, unroll=unroll),
        grid=(b, n_heads, s // tq),
        in_specs=[pl.BlockSpec(memory_space=pltpu.SMEM),
                  pl.BlockSpec((4, DIFF_DIM), lambda bi, h, i: (0, 0)),
                  pl.BlockSpec((1, hw), lambda bi, h, i: (0, 0)),
                  pl.BlockSpec((1, tq, hw), lambda bi, h, i: (bi, i, h)),
                  pl.BlockSpec((1, s, hw), lambda bi, h, i: (bi, 0, n_heads + h)),
                  pl.BlockSpec((1, s, hw), lambda bi, h, i: (bi, 0, 2 * n_heads + h))],
        out_specs=pl.BlockSpec((1, tq, hw), lambda bi, h, i: (bi, i, h)),
        out_shape=jax.ShapeDtypeStruct((b, s, n_heads * hw), BF16),
        compiler_params=_params("parallel", "parallel", "arbitrary"),
        name="diff_attn",
    )(slopes, c_lambda, c_subnorm.reshape(1, hw), proj, proj, proj)


def _router_body(h_ref, g_ref, wrt_ref, aff_ref):
    x = h_ref[...]
    ms = jnp.mean(x * x, axis=-1, keepdims=True)
    xn = x * lax.rsqrt(ms + EPS) * g_ref[...]
    logits = lax.dot_general(wrt_ref[...], xn, _NT, precision=lax.Precision.HIGHEST,
                             preferred_element_type=F32)
    m = jnp.max(logits, axis=0, keepdims=True)
    e = jnp.exp(logits - m)
    aff_ref[0] = e / jnp.sum(e, axis=0, keepdims=True)


def _router(h2d, g, w_router, b, s, tm=512):
    t, d = h2d.shape
    tm = min(tm, s)
    spb = s // tm
    ne = w_router.shape[1]
    return pl.pallas_call(
        _router_body,
        grid=(t // tm,),
        in_specs=[pl.BlockSpec((tm, d), lambda i: (i, 0)),
                  pl.BlockSpec((1, d), lambda i: (0, 0)),
                  pl.BlockSpec((ne, d), lambda i: (0, 0))],
        out_specs=pl.BlockSpec((1, ne, tm), lambda i: (i // spb, 0, i % spb)),
        out_shape=jax.ShapeDtypeStruct((b, ne, s), F32),
        compiler_params=_params("parallel"),
        name="router",
    )(h2d, g.reshape(1, d), w_router.T)


def _cumsum_lanes(mask, tri):
    rows, n = mask.shape
    carry = jnp.zeros((rows, 1), F32)
    out = []
    for blk in range(n // 128):
        c = jnp.dot(mask[:, blk * 128:(blk + 1) * 128].astype(BF16), tri, preferred_element_type=F32) + carry
        out.append(c)
        carry = c[:, 127:128]
    return jnp.concatenate(out, axis=1)


def _select_body(aff_ref, pos_ref, *, cap):
    a = aff_ref[0]
    bits = pltpu.bitcast(a, jnp.int32)
    ne = a.shape[0]
    thr = jnp.zeros((ne, 1), jnp.int32)
    for bit in range(30, -1, -1):
        cand = thr | (1 << bit)
        cnt = jnp.sum((bits >= cand).astype(jnp.int32), axis=1, keepdims=True)
        thr = jnp.where(cnt >= cap, cand, thr)
    gt = bits > thr
    eq = bits == thr
    need = (cap - jnp.sum(gt.astype(jnp.int32), axis=1, keepdims=True)).astype(F32)
    r = lax.broadcasted_iota(jnp.int32, (128, 128), 0)
    c = lax.broadcasted_iota(jnp.int32, (128, 128), 1)
    tri = (r <= c).astype(BF16)
    sel = gt | (eq & (_cumsum_lanes(eq.astype(F32), tri) <= need))
    pos = _cumsum_lanes(sel.astype(F32), tri) - 1.0
    pos_ref[0] = jnp.where(sel, pos, -1.0)


def _select(aff, cap):
    b, ne, s = aff.shape
    return pl.pallas_call(
        functools.partial(_select_body, cap=cap),
        grid=(b,),
        in_specs=[pl.BlockSpec((1, ne, s), lambda bi: (bi, 0, 0))],
        out_specs=pl.BlockSpec((1, ne, s), lambda bi: (bi, 0, 0)),
        out_shape=jax.ShapeDtypeStruct((b, ne, s), F32),
        compiler_params=_params("parallel"),
        name="select",
    )(aff)


def _compact_body(pos_ref, aff_ref, idx_ref, gate_ref, *, tc):
    c0 = pl.program_id(1) * tc
    pos = pos_ref[0]
    s = pos.shape[1]
    slot = (c0 + lax.broadcasted_iota(jnp.int32, (tc, s), 0)).astype(F32)
    tok = lax.broadcasted_iota(jnp.int32, (tc, s), 1)
    hit = pos == slot
    idx_ref[0] = jnp.sum(jnp.where(hit, tok, 0), axis=1, keepdims=True)
    gate_ref[0] = jnp.sum(jnp.where(hit, aff_ref[0], 0.0), axis=1, keepdims=True)


def _compact(pos, aff, cap, tc=128):
    b, ne, s = pos.shape
    tc = min(tc, cap)
    row = pl.BlockSpec((1, 1, s), lambda r, j: (r, 0, 0))
    col = pl.BlockSpec((1, tc, 1), lambda r, j: (r, j, 0))
    return pl.pallas_call(
        functools.partial(_compact_body, tc=tc),
        grid=(b * ne, cap // tc),
        in_specs=[row, row],
        out_specs=[col, col],
        out_shape=[jax.ShapeDtypeStruct((b * ne, cap, 1), jnp.int32),
                   jax.ShapeDtypeStruct((b * ne, cap, 1), F32)],
        compiler_params=_params("parallel", "parallel"),
        name="compact",
    )(pos.reshape(b * ne, 1, s), aff.reshape(b * ne, 1, s))


def _ffn_body(idx_ref, h_hbm, g_ref, gate_ref, wg_ref, wu_ref, wd_ref, o_ref, land, xg, hid, sems,
              *, cap, s_len, n_b, n_f, piece):
    e = pl.program_id(0)
    step = pl.program_id(1)
    ne = pl.num_programs(0)
    rows = n_b * cap
    n_pieces = rows // piece

    def row_copy(src_row, slot, r):
        return pltpu.make_async_copy(h_hbm.at[pl.ds(src_row, 1)], land.at[slot, pl.ds(r, 1)], sems.at[slot])

    def issue(p):
        bi, c0 = divmod(p * piece, cap)
        base = (bi * ne + e) * cap + c0

        def body(r, carry):
            row_copy(bi * s_len + idx_ref[base + r], p % 2, r).start()
            return carry

        lax.fori_loop(0, piece, body, 0, unroll=8)

    def drain(p):
        def body(r, carry):
            row_copy(0, p % 2, r).wait()
            return carry

        lax.fori_loop(0, piece, body, 0, unroll=8)

    @pl.when(step == 0)
    def _gather_and_norm():
        issue(0)
        for p in range(n_pieces):
            if p + 1 < n_pieces:
                issue(p + 1)
            drain(p)
            x = land[p % 2]
            ms = jnp.mean(x * x, axis=-1, keepdims=True)
            xg[p * piece:(p + 1) * piece, :] = (x * lax.rsqrt(ms + EPS) * g_ref[...]).astype(BF16)

    @pl.when(step < n_f)
    def _up():
        x = xg[...]
        gt = jnp.dot(x, wg_ref[0, 0].astype(BF16), preferred_element_type=F32)
        up = jnp.dot(x, wu_ref[0, 0].astype(BF16), preferred_element_type=F32)
        hid[step] = (gt * jax.nn.sigmoid(gt) * up).astype(BF16)

    @pl.when(step >= n_f)
    def _down():
        hcat = jnp.concatenate([hid[f] for f in range(n_f)], axis=1)
        y = jnp.dot(hcat, wd_ref[0, 0].astype(BF16), preferred_element_type=F32)
        o_ref[0] = y * gate_ref[:, 0].reshape(rows, 1)


def _expert_ffn(idx_flat, h2d, g, gates, w_gate, w_up, w_down, layer, b, s, cap, tf=256, tn=512, piece=128):
    t, d = h2d.shape
    _, ne, _, dff = w_gate.shape
    tf = min(tf, dff)
    tn = min(tn, d)
    n_f = dff // tf
    n_d = d // tn
    rows = b * cap
    piece = min(piece, cap)
    grid_spec = pltpu.PrefetchScalarGridSpec(
        num_scalar_prefetch=1,
        grid=(ne, n_f + n_d),
        in_specs=[pl.BlockSpec(memory_space=pl.ANY),
                  pl.BlockSpec((1, d), lambda e, st, idx: (0, 0)),
                  pl.BlockSpec((b, 1, cap, 1), lambda e, st, idx: (0, e, 0, 0)),
                  pl.BlockSpec((1, 1, d, tf), lambda e, st, idx: (layer, e, 0, jnp.minimum(st, n_f - 1))),
                  pl.BlockSpec((1, 1, d, tf), lambda e, st, idx: (layer, e, 0, jnp.minimum(st, n_f - 1))),
                  pl.BlockSpec((1, 1, dff, tn), lambda e, st, idx: (layer, e, 0, jnp.maximum(st - n_f, 0)))],
        out_specs=pl.BlockSpec((1, rows, tn), lambda e, st, idx: (e, 0, jnp.maximum(st - n_f, 0))),
        scratch_shapes=[pltpu.VMEM((2, piece, d), F32), pltpu.VMEM((rows, d), BF16),
                        pltpu.VMEM((n_f, rows, tf), BF16), pltpu.SemaphoreType.DMA((2,))],
    )
    return pl.pallas_call(
        functools.partial(_ffn_body, cap=cap, s_len=s, n_b=b, n_f=n_f, piece=piece),
        grid_spec=grid_spec,
        out_shape=jax.ShapeDtypeStruct((ne, rows, d), F32),
        compiler_params=_params("arbitrary", "arbitrary"),
        name="expert_ffn",
    )(idx_flat, h2d, g.reshape(1, d), gates.reshape(b, ne, cap, 1), w_gate, w_up, w_down)


def _combine_body(idx_ref, h_hbm, y_ref, o_ref, sem, *, cap, ne, rows_per_group, tn):
    b = pl.program_id(0)
    j = pl.program_id(1)
    e = pl.program_id(2)

    @pl.when(e == 0)
    def _init():
        cp = pltpu.make_async_copy(h_hbm.at[b, :, pl.ds(pl.multiple_of(j * tn, tn), tn)], o_ref.at[0], sem)
        cp.start()
        cp.wait()

    base = (b * ne + e) * cap

    def group(gi, carry):
        c0 = pl.multiple_of(gi * rows_per_group, rows_per_group)
        toks = [idx_ref[base + c0 + r] for r in range(rows_per_group)]
        cur = [o_ref[0, pl.ds(toks[r], 1), :] for r in range(rows_per_group)]
        upd = [cur[r] + y_ref[0, pl.ds(c0 + r, 1), :] for r in range(rows_per_group)]
        for r in range(rows_per_group):
            o_ref[0, pl.ds(toks[r], 1), :] = upd[r]
        return carry

    lax.fori_loop(0, cap // rows_per_group, group, 0)


def _combine(idx_flat, h, y, cap, tn=1024, rows_per_group=4):
    b, s, d = h.shape
    ne = y.shape[0]
    tn = min(tn, d)
    rows_per_group = min(rows_per_group, cap)
    grid_spec = pltpu.PrefetchScalarGridSpec(
        num_scalar_prefetch=1,
        grid=(b, d // tn, ne),
        in_specs=[pl.BlockSpec(memory_space=pl.ANY),
                  pl.BlockSpec((1, cap, tn), lambda bi, j, e, idx: (e, bi, j))],
        out_specs=pl.BlockSpec((1, s, tn), lambda bi, j, e, idx: (bi, 0, j)),
        scratch_shapes=[pltpu.SemaphoreType.DMA(())],
    )
    return pl.pallas_call(
        functools.partial(_combine_body, cap=cap, ne=ne, rows_per_group=rows_per_group, tn=tn),
        grid_spec=grid_spec,
        out_shape=jax.ShapeDtypeStruct((b, s, d), F32),
        compiler_params=_params("arbitrary", "arbitrary", "arbitrary"),
        name="combine",
    )(idx_flat, h, y)


def _moe(h, g, w_router, w_gate, w_up, w_down, layer):
    b, s, d = h.shape
    ne = w_router.shape[1]
    cap = CAPACITY_FACTOR * s // ne
    h2d = h.reshape(b * s, d)
    aff = _router(h2d, g, w_router, b, s)
    pos = _select(aff, cap)
    idx, gates = _compact(pos, aff, cap)
    idx_flat = idx.reshape(b * ne * cap)
    y = _expert_ffn(idx_flat, h2d, g, gates, w_gate, w_up, w_down, layer, b, s, cap)
    return _combine(idx_flat, h, y, cap)


def _alibi_slopes(n):
    return 2.0 ** (-8.0 * jnp.arange(1, n + 1, dtype=F32) / n)


def _mm_tiles(m, n):
    tm = next(t for t in (512, 256, 128, 8) if m % t == 0)
    tn = next(t for t in (512, 256, 128) if n % t == 0)
    return tm, tn


def _even_mixer(h, g, w_in, w_out, sink, qnorm, knorm):
    b, s, d = h.shape
    t = b * s
    n_heads = d // HEAD_DIM
    ha, hb = n_heads // 2, n_heads // 2
    kva, kvb = ha // 4, hb // 4
    q_a, kv_a, q_b, kv_b = ha * HEAD_DIM, kva * HEAD_DIM, hb * HEAD_DIM, kvb * HEAD_DIM
    h2d = h.reshape(t, d)
    hn = _rmsnorm(h2d, g, BF16)
    w_total = w_in.shape[1]
    tm, tn = _mm_tiles(t, w_total)
    proj = _matmul([hn], w_in, None, BF16, tm, tn, "in_proj_even").reshape(b, s, w_total)
    out_a = _window_attention(proj, _alibi_slopes(ha), sink, n_heads=ha, n_kv=kva,
                              q_col=0, k_col=q_a, v_col=q_a + kv_a)
    qb0 = q_a + 2 * kv_a
    qk = _qk_prep(proj, jnp.stack([qnorm, knorm]), q_col=qb0, n_q_heads=hb, n_k_heads=kvb)
    out_b = _grid_attention(qk, proj, n_heads=hb, n_kv=kvb, v_col=qb0 + q_b + kv_b)
    tm, tn = _mm_tiles(t, d)
    out = _matmul([out_a.reshape(t, q_a), out_b.reshape(t, q_b)], w_out, h2d, F32, tm, tn, "out_proj_even")
    return out.reshape(b, s, d)


def _odd_mixer(h, g, w_in, w_out, c_lambda, c_subnorm, lambda_init):
    b, s, d = h.shape
    t = b * s
    n_heads = d // (2 * DIFF_DIM)
    h2d = h.reshape(t, d)
    hn = _rmsnorm(h2d, g, BF16)
    w_total = w_in.shape[1]
    tm, tn = _mm_tiles(t, w_total)
    col_scale = jnp.where(jnp.arange(w_total) < d, DIFF_DIM ** -0.5 * LOG2E, 1.0).astype(F32)
    proj = _matmul([hn], w_in, None, BF16, tm, tn, "in_proj_odd", col_scale).reshape(b, s, w_total)
    mix = _diff_attention(proj, _alibi_slopes(n_heads), c_lambda, c_subnorm, lambda_init, n_heads=n_heads)
    tm, tn = _mm_tiles(t, d)
    out = _matmul([mix.reshape(t, d)], w_out, h2d, F32, tm, tn, "out_proj_odd")
    return out.reshape(b, s, d)


def kernel(x, norm_mix, norm_ffn, norm_final, w_in_even, w_out_even, sink_a, qnorm_b, knorm_b,
           w_in_odd, w_out_odd, c_lambda, c_subnorm, w_router, w_gate, w_up, w_down):
    b, s, d = x.shape
    depth = norm_mix.shape[0]
    h = x
    for layer in range(depth):
        i = layer // 2
        if layer % 2 == 0:
            h = _even_mixer(h, norm_mix[layer], w_in_even[i], w_out_even[i], sink_a[i], qnorm_b[i], knorm_b[i])
        else:
            lambda_init = 0.8 - 0.6 * math.exp(-0.3 * layer)
            h = _odd_mixer(h, norm_mix[layer], w_in_odd[i], w_out_odd[i], c_lambda[i], c_subnorm[i], lambda_init)
        h = _moe(h, norm_ffn[layer], w_router[layer], w_gate, w_up, w_down, layer)
    return _rmsnorm(h.reshape(b * s, d), norm_final, F32).reshape(b, s, d)
```

```python
import functools
import math

import jax
import jax.numpy as jnp
from jax import lax
from jax.experimental import pallas as pl
from jax.experimental.pallas import tpu as pltpu

F32 = jnp.float32
BF16 = jnp.bfloat16

HEAD_DIM = 128
WINDOW = 128
BLOCK = 128
GRID_W = 64
ROPE_THETA = 10000.0
DIFF_DIM = 128
N_EXPERTS = 16
CAPACITY_FACTOR = 2
EPS = 1e-6
NEG = -1e30
LOG2E = math.log2(math.e)

VMEM_LIMIT_BYTES = 56 * 1024 * 1024

_NT = (((1,), (1,)), ((), ()))


def _params(*sem):
    return pltpu.CompilerParams(dimension_semantics=sem, vmem_limit_bytes=VMEM_LIMIT_BYTES)


def _rmsnorm_body(x_ref, g_ref, o_ref):
    x = x_ref[...]
    ms = jnp.mean(x * x, axis=-1, keepdims=True)
    o_ref[...] = (x * lax.rsqrt(ms + EPS) * g_ref[...]).astype(o_ref.dtype)


def _rmsnorm(x2d, g, out_dtype, tm=256):
    t, d = x2d.shape
    return pl.pallas_call(
        _rmsnorm_body,
        grid=(t // tm,),
        in_specs=[pl.BlockSpec((tm, d), lambda i: (i, 0)),
                  pl.BlockSpec((1, d), lambda i: (0, 0))],
        out_specs=pl.BlockSpec((tm, d), lambda i: (i, 0)),
        out_shape=jax.ShapeDtypeStruct((t, d), out_dtype),
        compiler_params=_params("parallel"),
        name="rmsnorm",
    )(x2d, g.reshape(1, d))


def _mm_body(*refs, n_pairs, has_res, has_scale):
    n_in = 2 * n_pairs + int(has_res) + int(has_scale)
    o_ref = refs[n_in]
    wbf = refs[n_in + 1:]

    @pl.when(pl.program_id(1) == 0)
    def _cast():
        for p in range(n_pairs):
            wbf[p][...] = refs[2 * p + 1][...].astype(BF16)

    acc = None
    for p in range(n_pairs):
        d = jnp.dot(refs[2 * p][...], wbf[p][...], preferred_element_type=F32)
        acc = d if acc is None else acc + d
    if has_res:
        acc = acc + refs[2 * n_pairs][...]
    if has_scale:
        acc = acc * refs[n_in - 1][...]
    o_ref[...] = acc.astype(o_ref.dtype)


def _matmul(a_list, w, res, out_dtype, tm, tn, name, col_scale=None):
    m = a_list[0].shape[0]
    n = w.shape[1]
    in_specs, args = [], []
    kp = a_list[0].shape[1]
    assert all(a.shape == (m, kp) for a in a_list) and w.shape[0] == kp * len(a_list)
    for part, a in enumerate(a_list):
        in_specs.append(pl.BlockSpec((tm, kp), lambda j, i: (i, 0)))
        in_specs.append(pl.BlockSpec((kp, tn), functools.partial(lambda j, i, r: (r, j), r=part)))
        args += [a, w]
    if res is not None:
        in_specs.append(pl.BlockSpec((tm, tn), lambda j, i: (i, j)))
        args.append(res)
    if col_scale is not None:
        in_specs.append(pl.BlockSpec((1, tn), lambda j, i: (0, j)))
        args.append(col_scale.reshape(1, n))
    return pl.pallas_call(
        functools.partial(_mm_body, n_pairs=len(a_list), has_res=res is not None,
                          has_scale=col_scale is not None),
        grid=(n // tn, m // tm),
        in_specs=in_specs,
        out_specs=pl.BlockSpec((tm, tn), lambda j, i: (i, j)),
        out_shape=jax.ShapeDtypeStruct((m, n), out_dtype),
        scratch_shapes=[pltpu.VMEM((kp, tn), BF16) for _ in a_list],
        compiler_params=_params("arbitrary", "arbitrary"),
        name=name,
    )(*args)


def _win_body(slope_ref, sink_ref, q_ref, kp_ref, kc_ref, kn_ref, vp_ref, vc_ref, vn_ref, o_ref,
              *, nb, group):
    kv = pl.program_id(1)
    pair = pl.program_id(2)
    k_blocks = [kp_ref[0], kc_ref[0, :BLOCK], kc_ref[0, BLOCK:], kn_ref[0]]
    v_blocks = [vp_ref[0], vc_ref[0, :BLOCK], vc_ref[0, BLOCK:], vn_ref[0]]
    qi = lax.broadcasted_iota(jnp.int32, (BLOCK, 3 * BLOCK), 0)
    kj = lax.broadcasted_iota(jnp.int32, (BLOCK, 3 * BLOCK), 1)
    arel = jnp.abs(BLOCK + qi - kj)
    arel_f = arel.astype(F32)
    heads = [kv * group + g for g in range(group)]
    sk = jnp.concatenate([jnp.full((BLOCK, 1), sink_ref[h] * LOG2E, F32) for h in heads], axis=0)
    for half in range(2):
        n = 2 * pair + half
        kb = jnp.concatenate(k_blocks[half:half + 3], axis=0)
        vb = jnp.concatenate(v_blocks[half:half + 3], axis=0)
        kpos = (n - 1) * BLOCK + kj
        valid = (arel <= WINDOW) & (kpos >= 0) & (kpos < nb * BLOCK)
        q = q_ref[0, half * BLOCK:(half + 1) * BLOCK, :]
        qs = jnp.concatenate([q[:, g * HEAD_DIM:(g + 1) * HEAD_DIM] for g in range(group)], axis=0)
        bias = jnp.concatenate([jnp.where(valid, -(slope_ref[h] * LOG2E) * arel_f, NEG) for h in heads], axis=0)
        s = lax.dot_general(qs, kb, _NT, preferred_element_type=F32) + bias
        m = jnp.maximum(jnp.max(s, axis=-1, keepdims=True), sk)
        p = jnp.exp2(s - m)
        den = jnp.sum(p, axis=-1, keepdims=True) + jnp.exp2(sk - m)
        o = jnp.dot(p.astype(BF16), vb, preferred_element_type=F32) / den
        for g in range(group):
            o_ref[0, half * BLOCK:(half + 1) * BLOCK, g * HEAD_DIM:(g + 1) * HEAD_DIM] = (
                o[g * BLOCK:(g + 1) * BLOCK].astype(o_ref.dtype))


def _window_attention(proj, slopes, sink, *, n_heads, n_kv, q_col, k_col, v_col):
    b, s, _ = proj.shape
    nb = s // BLOCK
    group = n_heads // n_kv
    qw = group * HEAD_DIM
    qb0, kb0, vb0 = q_col // qw, k_col // HEAD_DIM, v_col // HEAD_DIM
    assert q_col % qw == 0 and k_col % HEAD_DIM == 0 and v_col % HEAD_DIM == 0

    assert nb % 2 == 0

    def edge_spec(col0, shift):
        def imap(bi, kv, pair):
            return (bi, jnp.clip(2 * pair + shift, 0, nb - 1), col0 + kv)
        return pl.BlockSpec((1, BLOCK, HEAD_DIM), imap)

    def pair_spec(col0):
        return pl.BlockSpec((1, 2 * BLOCK, HEAD_DIM), lambda bi, kv, pair: (bi, pair, col0 + kv))

    smem = pl.BlockSpec(memory_space=pltpu.SMEM)
    return pl.pallas_call(
        functools.partial(_win_body, nb=nb, group=group),
        grid=(b, n_kv, nb // 2),
        in_specs=[smem, smem,
                  pl.BlockSpec((1, 2 * BLOCK, qw), lambda bi, kv, pair: (bi, pair, qb0 + kv)),
                  edge_spec(kb0, -1), pair_spec(kb0), edge_spec(kb0, 2),
                  edge_spec(vb0, -1), pair_spec(vb0), edge_spec(vb0, 2)],
        out_specs=pl.BlockSpec((1, 2 * BLOCK, qw), lambda bi, kv, pair: (bi, pair, kv)),
        out_shape=jax.ShapeDtypeStruct((b, s, n_heads * HEAD_DIM), BF16),
        compiler_params=_params("parallel", "parallel", "parallel"),
        name="window_attn",
    )(slopes, sink, proj, proj, proj, proj, proj, proj, proj)


def _qkprep_body(x_ref, g_ref, cos_ref, sin_ref, o_ref, *, n_qchunks, heads_per_chunk, scale):
    j = pl.program_id(2)
    is_k = j >= n_qchunks
    gain = jnp.where(is_k, g_ref[1:2, :], g_ref[0:1, :])
    mult = jnp.where(is_k, 1.0, scale)
    cos = cos_ref[...]
    sin = sin_ref[...]
    lane = lax.broadcasted_iota(jnp.int32, cos.shape, 1)
    first = (lane % (HEAD_DIM // 2)) < (HEAD_DIM // 4)
    for hd in range(heads_per_chunk):
        x = x_ref[0, :, hd * HEAD_DIM:(hd + 1) * HEAD_DIM].astype(F32)
        ms = jnp.mean(x * x, axis=-1, keepdims=True)
        y = x * lax.rsqrt(ms + EPS) * gain
        partner = jnp.where(first, pltpu.roll(y, HEAD_DIM - HEAD_DIM // 4, 1),
                            pltpu.roll(y, HEAD_DIM // 4, 1))
        r = y * cos + partner * sin
        o_ref[0, :, hd * HEAD_DIM:(hd + 1) * HEAD_DIM] = (r * mult).astype(o_ref.dtype)


def _rope_tables(s):
    rows = s // GRID_W
    row = jnp.repeat(jnp.arange(rows), GRID_W).astype(F32)
    col = jnp.tile(jnp.arange(GRID_W), rows).astype(F32)
    quarter = HEAD_DIM // 4
    inv = ROPE_THETA ** (-jnp.arange(quarter, dtype=F32) / quarter)
    ang_r = row[:, None] * inv[None, :]
    ang_c = col[:, None] * inv[None, :]
    cos = jnp.concatenate([jnp.cos(ang_r)] * 2 + [jnp.cos(ang_c)] * 2, axis=-1)
    sin = jnp.concatenate([-jnp.sin(ang_r), jnp.sin(ang_r), -jnp.sin(ang_c), jnp.sin(ang_c)], axis=-1)
    return cos, sin


def _qk_prep(proj, gains, *, q_col, n_q_heads, n_k_heads, ts=256):
    b, s, _ = proj.shape
    cw = n_k_heads * HEAD_DIM
    assert q_col % cw == 0 and (n_q_heads * HEAD_DIM) % cw == 0
    n_qchunks = n_q_heads * HEAD_DIM // cw
    c0 = q_col // cw
    cos, sin = _rope_tables(s)
    return pl.pallas_call(
        functools.partial(_qkprep_body, n_qchunks=n_qchunks, heads_per_chunk=n_k_heads,
                          scale=HEAD_DIM ** -0.5 * LOG2E),
        grid=(b, s // ts, n_qchunks + 1),
        in_specs=[pl.BlockSpec((1, ts, cw), lambda bi, i, j: (bi, i, c0 + j)),
                  pl.BlockSpec((2, HEAD_DIM), lambda bi, i, j: (0, 0)),
                  pl.BlockSpec((ts, HEAD_DIM), lambda bi, i, j: (i, 0)),
                  pl.BlockSpec((ts, HEAD_DIM), lambda bi, i, j: (i, 0))],
        out_specs=pl.BlockSpec((1, ts, cw), lambda bi, i, j: (bi, i, j)),
        out_shape=jax.ShapeDtypeStruct((b, s, (n_qchunks + 1) * cw), BF16),
        compiler_params=_params("parallel", "parallel", "parallel"),
        name="qk_prep",
    )(proj, gains, cos, sin)


def _lane_groups(x):
    return [x[:, g * 128:(g + 1) * 128] for g in range(x.shape[1] // 128)]


def _grid_attn_body(q_ref, k_ref, v_ref, o_ref, *, tq, group, kc, s_len, unroll):
    q = q_ref[0]
    qs = jnp.concatenate([q[:, g * HEAD_DIM:(g + 1) * HEAD_DIM] for g in range(group)], axis=0)
    rows = group * tq

    def step(j, carry):
        m, lp, acc = carry
        off = pl.multiple_of(j * kc, kc)
        sg = _lane_groups(lax.dot_general(qs, k_ref[0, pl.ds(off, kc), :], _NT, preferred_element_type=F32))
        cmax = jnp.max(functools.reduce(jnp.maximum, sg), axis=-1, keepdims=True)
        m_new = jnp.maximum(m, jnp.broadcast_to(cmax, (rows, 128)))
        alpha = jnp.exp2(m - m_new)
        ps = [jnp.exp2(x - m_new) for x in sg]
        lp = alpha * lp + functools.reduce(jnp.add, ps)
        p = jnp.concatenate([pg.astype(BF16) for pg in ps], axis=1)
        acc = alpha * acc + jnp.dot(p, v_ref[0, pl.ds(off, kc), :], preferred_element_type=F32)
        return m_new, lp, acc

    assert HEAD_DIM == 128
    init = (jnp.full((rows, 128), NEG, F32), jnp.zeros((rows, 128), F32), jnp.zeros((rows, HEAD_DIM), F32))
    _, lp, acc = lax.fori_loop(0, s_len // kc, step, init, unroll=unroll)
    o = acc / jnp.sum(lp, axis=-1, keepdims=True)
    for g in range(group):
        o_ref[0, :, g * HEAD_DIM:(g + 1) * HEAD_DIM] = o[g * tq:(g + 1) * tq].astype(o_ref.dtype)


def _grid_attention(qk, proj, *, n_heads, n_kv, v_col, tq=256, kc=512, unroll=2):
    b, s, _ = qk.shape
    group = n_heads // n_kv
    qw = group * HEAD_DIM
    kc = min(kc, s)
    vb0 = v_col // HEAD_DIM
    return pl.pallas_call(
        functools.partial(_grid_attn_body, tq=tq, group=group, kc=kc, s_len=s, unroll=unroll),
        grid=(b, n_kv, s // tq),
        in_specs=[pl.BlockSpec((1, tq, qw), lambda bi, kv, i: (bi, i, kv)),
                  pl.BlockSpec((1, s, HEAD_DIM), lambda bi, kv, i: (bi, 0, n_heads + kv)),
                  pl.BlockSpec((1, s, HEAD_DIM), lambda bi, kv, i: (bi, 0, vb0 + kv))],
        out_specs=pl.BlockSpec((1, tq, qw), lambda bi, kv, i: (bi, i, kv)),
        out_shape=jax.ShapeDtypeStruct((b, s, n_heads * HEAD_DIM), BF16),
        compiler_params=_params("parallel", "parallel", "arbitrary"),
        name="grid_attn",
    )(qk, qk, proj)


def _diff_attn_body(slope_ref, lamc_ref, sub_ref, q_ref, k_ref, v_ref, o_ref,
                    *, tq, kc, s_len, lambda_init, unroll):
    h = pl.program_id(1)
    r = pl.program_id(2)
    slope2 = slope_ref[h] * LOG2E
    q = q_ref[0]
    zq = jnp.zeros((tq, DIFF_DIM), q.dtype)
    qd = jnp.concatenate([jnp.concatenate([q[:, :DIFF_DIM], zq], axis=1),
                          jnp.concatenate([zq, q[:, DIFF_DIM:]], axis=1)], axis=0)
    rows = 2 * tq
    n_chunks = s_len // kc
    i0 = r * tq
    c_diag = i0 // kc
    kj = lax.broadcasted_iota(jnp.int32, (1, kc), 1).astype(F32)
    qi = (lax.broadcasted_iota(jnp.int32, (rows, 128), 0) % tq).astype(F32)

    def off_diag(c):
        return c + (c >= c_diag).astype(jnp.int32)

    def terms(cc):
        is_left = cc < c_diag
        gap = jnp.where(is_left, i0 - (cc + 1) * kc, cc * kc - i0 - tq).astype(F32)
        colterm = -slope2 * jnp.where(is_left, kc - kj, kj)
        rowterm = -slope2 * (jnp.where(is_left, qi, tq - qi) + gap)
        return colterm, rowterm

    def diag_bias():
        in_chunk = (i0 - c_diag * kc).astype(F32)
        return -slope2 * jnp.abs(qi[:, :1] + in_chunk - kj)

    def probs(cc, bias, rowterm, m, lp):
        off = pl.multiple_of(cc * kc, kc)
        ug = _lane_groups(lax.dot_general(qd, k_ref[0, pl.ds(off, kc), :], _NT,
                                          preferred_element_type=F32) + bias)
        cmax = jnp.max(functools.reduce(jnp.maximum, ug), axis=-1, keepdims=True)
        m_new = jnp.maximum(m, jnp.broadcast_to(cmax, (rows, 128)) + rowterm)
        alpha = jnp.exp2(m - m_new)
        shift = m_new - rowterm
        ps = [jnp.exp2(x - shift) for x in ug]
        lp = alpha * lp + functools.reduce(jnp.add, ps)
        p = jnp.concatenate([pg.astype(BF16) for pg in ps], axis=1)
        return m_new, lp, alpha, p

    def weighted_values(cc, alpha, p, acc):
        off = pl.multiple_of(cc * kc, kc)
        return (jnp.concatenate([alpha * ag for ag in _lane_groups(acc)], axis=1)
                + jnp.dot(p, v_ref[0, pl.ds(off, kc), :], preferred_element_type=F32))

    def step(c, carry):
        m, lp, acc, alpha, p, prev = carry
        cc = off_diag(c)
        colterm, rowterm = terms(cc)
        m, lp, alpha_next, p_next = probs(cc, colterm, rowterm, m, lp)
        acc = weighted_values(prev, alpha, p, acc)
        return m, lp, acc, alpha_next, p_next, cc

    m0 = jnp.full((rows, 128), NEG, F32)
    m, lp, alpha, p = probs(c_diag, diag_bias(), 0.0, m0, jnp.zeros((rows, 128), F32))
    carry = (m, lp, jnp.zeros((rows, 2 * DIFF_DIM), F32), alpha, p, c_diag)
    _, lp, acc, alpha, p, last = lax.fori_loop(0, n_chunks - 1, step, carry, unroll=unroll)
    acc = weighted_values(last, alpha, p, acc)
    l = jnp.sum(lp, axis=-1, keepdims=True)
    a1, a2, l1, l2 = acc[:tq], acc[tq:], l[:tq], l[tq:]
    lf = lamc_ref[...]
    lam = (jnp.exp(jnp.sum(lf[0:1] * lf[1:2], axis=-1, keepdims=True))
           - jnp.exp(jnp.sum(lf[2:3] * lf[3:4], axis=-1, keepdims=True)) + lambda_init)
    o = a1 / l1 - lam * (a2 / l2)
    ms = jnp.mean(o * o, axis=-1, keepdims=True)
    o = o * lax.rsqrt(ms + EPS) * sub_ref[...] * (1.0 - lambda_init)
    o_ref[0] = o.astype(o_ref.dtype)


def _diff_attention(proj, slopes, c_lambda, c_subnorm, lambda_init, *, n_heads, tq=256, kc=512, unroll=2):
    b, s, _ = proj.shape
    hw = 2 * DIFF_DIM
    kc = min(kc, s)
    tq = min(tq, kc)
    assert kc % tq == 0 and s % kc == 0
    return pl.pallas_call(
        functools.partial(_diff_attn_body, tq=tq, kc=kc, s_len=s, lambda_init=lambda_init, unroll=unroll),
        grid=(b, n_heads, s // tq),
        in_specs=[pl.BlockSpec(memory_space=pltpu.SMEM),
                  pl.BlockSpec((4, DIFF_DIM), lambda bi, h, i: (0, 0)),
                  pl.BlockSpec((1, hw), lambda bi, h, i: (0, 0)),
                  pl.BlockSpec((1, tq, hw), lambda bi, h, i: (bi, i, h)),
                  pl.BlockSpec((1, s, hw), lambda bi, h, i: (bi, 0, n_heads + h)),
                  pl.BlockSpec((1, s, hw), lambda bi, h, i: (bi, 0, 2 * n_heads + h))],
        out_specs=pl.BlockSpec((1, tq, hw), lambda bi, h, i: (bi, i, h)),
        out_shape=jax.ShapeDtypeStruct((b, s, n_heads * hw), BF16),
        compiler_params=_params("parallel", "parallel", "arbitrary"),
        name="diff_attn",
    )(slopes, c_lambda, c_subnorm.reshape(1, hw), proj, proj, proj)


def _router_body(h_ref, g_ref, wrt_ref, aff_ref):
    x = h_ref[...]
    ms = jnp.mean(x * x, axis=-1, keepdims=True)
    xn = x * lax.rsqrt(ms + EPS) * g_ref[...]
    logits = lax.dot_general(wrt_ref[...], xn, _NT, precision=lax.Precision.HIGHEST,
                             preferred_element_type=F32)
    m = jnp.max(logits, axis=0, keepdims=True)
    e = jnp.exp(logits - m)
    aff_ref[0] = e / jnp.sum(e, axis=0, keepdims=True)


def _router(h2d, g, w_router, b, s, tm=512):
    t, d = h2d.shape
    tm = min(tm, s)
    spb = s // tm
    ne = w_router.shape[1]
    return pl.pallas_call(
        _router_body,
        grid=(t // tm,),
        in_specs=[pl.BlockSpec((tm, d), lambda i: (i, 0)),
                  pl.BlockSpec((1, d), lambda i: (0, 0)),
                  pl.BlockSpec((ne, d), lambda i: (0, 0))],
        out_specs=pl.BlockSpec((1, ne, tm), lambda i: (i // spb, 0, i % spb)),
        out_shape=jax.ShapeDtypeStruct((b, ne, s), F32),
        compiler_params=_params("parallel"),
        name="router",
    )(h2d, g.reshape(1, d), w_router.T)


def _cumsum_lanes(mask, tri):
    rows, n = mask.shape
    carry = jnp.zeros((rows, 1), F32)
    out = []
    for blk in range(n // 128):
        c = jnp.dot(mask[:, blk * 128:(blk + 1) * 128].astype(BF16), tri, preferred_element_type=F32) + carry
        out.append(c)
        carry = c[:, 127:128]
    return jnp.concatenate(out, axis=1)


def _select_body(aff_ref, pos_ref, *, cap):
    a = aff_ref[0]
    bits = pltpu.bitcast(a, jnp.int32)
    ne = a.shape[0]
    thr = jnp.zeros((ne, 1), jnp.int32)
    for bit in range(30, -1, -1):
        cand = thr | (1 << bit)
        cnt = jnp.sum((bits >= cand).astype(jnp.int32), axis=1, keepdims=True)
        thr = jnp.where(cnt >= cap, cand, thr)
    gt = bits > thr
    eq = bits == thr
    need = (cap - jnp.sum(gt.astype(jnp.int32), axis=1, keepdims=True)).astype(F32)
    r = lax.broadcasted_iota(jnp.int32, (128, 128), 0)
    c = lax.broadcasted_iota(jnp.int32, (128, 128), 1)
    tri = (r <= c).astype(BF16)
    sel = gt | (eq & (_cumsum_lanes(eq.astype(F32), tri) <= need))
    pos = _cumsum_lanes(sel.astype(F32), tri) - 1.0
    pos_ref[0] = jnp.where(sel, pos, -1.0)


def _select(aff, cap):
    b, ne, s = aff.shape
    return pl.pallas_call(
        functools.partial(_select_body, cap=cap),
        grid=(b,),
        in_specs=[pl.BlockSpec((1, ne, s), lambda bi: (bi, 0, 0))],
        out_specs=pl.BlockSpec((1, ne, s), lambda bi: (bi, 0, 0)),
        out_shape=jax.ShapeDtypeStruct((b, ne, s), F32),
        compiler_params=_params("parallel"),
        name="select",
    )(aff)


def _compact_body(pos_ref, aff_ref, idx_ref, gate_ref, *, tc):
    c0 = pl.program_id(1) * tc
    pos = pos_ref[0]
    s = pos.shape[1]
    slot = (c0 + lax.broadcasted_iota(jnp.int32, (tc, s), 0)).astype(F32)
    tok = lax.broadcasted_iota(jnp.int32, (tc, s), 1)
    hit = pos == slot
    idx_ref[0] = jnp.sum(jnp.where(hit, tok, 0), axis=1, keepdims=True)
    gate_ref[0] = jnp.sum(jnp.where(hit, aff_ref[0], 0.0), axis=1, keepdims=True)


def _compact(pos, aff, cap, tc=128):
    b, ne, s = pos.shape
    tc = min(tc, cap)
    row = pl.BlockSpec((1, 1, s), lambda r, j: (r, 0, 0))
    col = pl.BlockSpec((1, tc, 1), lambda r, j: (r, j, 0))
    return pl.pallas_call(
        functools.partial(_compact_body, tc=tc),
        grid=(b * ne, cap // tc),
        in_specs=[row, row],
        out_specs=[col, col],
        out_shape=[jax.ShapeDtypeStruct((b * ne, cap, 1), jnp.int32),
                   jax.ShapeDtypeStruct((b * ne, cap, 1), F32)],
        compiler_params=_params("parallel", "parallel"),
        name="compact",
    )(pos.reshape(b * ne, 1, s), aff.reshape(b * ne, 1, s))


def _ffn_body(idx_ref, h_hbm, g_ref, gate_ref, wg_ref, wu_ref, wd_ref, o_ref, land, xg, hid, sems,
              *, cap, s_len, n_b, n_f, piece):
    e = pl.program_id(0)
    step = pl.program_id(1)
    ne = pl.num_programs(0)
    rows = n_b * cap
    n_pieces = rows // piece

    def row_copy(src_row, slot, r):
        return pltpu.make_async_copy(h_hbm.at[pl.ds(src_row, 1)], land.at[slot, pl.ds(r, 1)], sems.at[slot])

    def issue(p):
        bi, c0 = divmod(p * piece, cap)
        base = (bi * ne + e) * cap + c0

        def body(r, carry):
            row_copy(bi * s_len + idx_ref[base + r], p % 2, r).start()
            return carry

        lax.fori_loop(0, piece, body, 0, unroll=8)

    def drain(p):
        def body(r, carry):
            row_copy(0, p % 2, r).wait()
            return carry

        lax.fori_loop(0, piece, body, 0, unroll=8)

    @pl.when(step == 0)
    def _gather_and_norm():
        issue(0)
        for p in range(n_pieces):
            if p + 1 < n_pieces:
                issue(p + 1)
            drain(p)
            x = land[p % 2]
            ms = jnp.mean(x * x, axis=-1, keepdims=True)
            xg[p * piece:(p + 1) * piece, :] = (x * lax.rsqrt(ms + EPS) * g_ref[...]).astype(BF16)

    @pl.when(step < n_f)
    def _up():
        x = xg[...]
        gt = jnp.dot(x, wg_ref[0, 0].astype(BF16), preferred_element_type=F32)
        up = jnp.dot(x, wu_ref[0, 0].astype(BF16), preferred_element_type=F32)
        hid[step] = (gt * jax.nn.sigmoid(gt) * up).astype(BF16)

    @pl.when(step >= n_f)
    def _down():
        hcat = jnp.concatenate([hid[f] for f in range(n_f)], axis=1)
        y = jnp.dot(hcat, wd_ref[0, 0].astype(BF16), preferred_element_type=F32)
        o_ref[0] = y * gate_ref[:, 0].reshape(rows, 1)


def _expert_ffn(idx_flat, h2d, g, gates, w_gate, w_up, w_down, layer, b, s, cap, tf=256, tn=512, piece=128):
    t, d = h2d.shape
    _, ne, _, dff = w_gate.shape
    tf = min(tf, dff)
    tn = min(tn, d)
    n_f = dff // tf
    n_d = d // tn
    rows = b * cap
    piece = min(piece, cap)
    grid_spec = pltpu.PrefetchScalarGridSpec(
        num_scalar_prefetch=1,
        grid=(ne, n_f + n_d),
        in_specs=[pl.BlockSpec(memory_space=pl.ANY),
                  pl.BlockSpec((1, d), lambda e, st, idx: (0, 0)),
                  pl.BlockSpec((b, 1, cap, 1), lambda e, st, idx: (0, e, 0, 0)),
                  pl.BlockSpec((1, 1, d, tf), lambda e, st, idx: (layer, e, 0, jnp.minimum(st, n_f - 1))),
                  pl.BlockSpec((1, 1, d, tf), lambda e, st, idx: (layer, e, 0, jnp.minimum(st, n_f - 1))),
                  pl.BlockSpec((1, 1, dff, tn), lambda e, st, idx: (layer, e, 0, jnp.maximum(st - n_f, 0)))],
        out_specs=pl.BlockSpec((1, rows, tn), lambda e, st, idx: (e, 0, jnp.maximum(st - n_f, 0))),
        scratch_shapes=[pltpu.VMEM((2, piece, d), F32), pltpu.VMEM((rows, d), BF16),
                        pltpu.VMEM((n_f, rows, tf), BF16), pltpu.SemaphoreType.DMA((2,))],
    )
    return pl.pallas_call(
        functools.partial(_ffn_body, cap=cap, s_len=s, n_b=b, n_f=n_f, piece=piece),
        grid_spec=grid_spec,
        out_shape=jax.ShapeDtypeStruct((ne, rows, d), F32),
        compiler_params=_params("arbitrary", "arbitrary"),
        name="expert_ffn",
    )(idx_flat, h2d, g.reshape(1, d), gates.reshape(b, ne, cap, 1), w_gate, w_up, w_down)


def _combine_body(idx_ref, h_hbm, y_ref, o_hbm, buf, gsem, ssem, *, cap, n_b):
    del h_hbm
    e = pl.program_id(0)
    b = pl.program_id(1)
    ne = pl.num_programs(0)
    step = e * n_b + b
    slot = step % 2
    base = (b * ne + e) * cap

    def gather_row(c, tok, sl):
        return pltpu.make_async_copy(o_hbm.at[b, pl.ds(tok, 1)], buf.at[sl, pl.ds(c, 1)], gsem.at[sl])

    def scatter_row(c, tok, sl):
        return pltpu.make_async_copy(buf.at[sl, pl.ds(c, 1)], o_hbm.at[b, pl.ds(tok, 1)], ssem.at[sl])

    def for_rows(fn):
        def body(c, carry):
            fn(c)
            return carry
        lax.fori_loop(0, cap, body, 0, unroll=8)

    def wait_previous_scatter():
        for_rows(lambda c: scatter_row(c, 0, 1 - slot).wait())

    if n_b == 1:
        pl.when(step > 0)(wait_previous_scatter)
    for_rows(lambda c: gather_row(c, idx_ref[base + c], slot).start())
    if n_b > 1:
        pl.when(step > 0)(wait_previous_scatter)
    for_rows(lambda c: gather_row(c, 0, slot).wait())
    buf[slot] = buf[slot] + y_ref[0]
    for_rows(lambda c: scatter_row(c, idx_ref[base + c], slot).start())

    @pl.when(step == ne * n_b - 1)
    def _drain():
        for_rows(lambda c: scatter_row(c, 0, slot).wait())


def _combine(idx_flat, h, y, cap):
    b, s, d = h.shape
    ne = y.shape[0]
    grid_spec = pltpu.PrefetchScalarGridSpec(
        num_scalar_prefetch=1,
        grid=(ne, b),
        in_specs=[pl.BlockSpec(memory_space=pl.ANY),
                  pl.BlockSpec((1, cap, d), lambda e, bi, idx: (e, bi, 0))],
        out_specs=pl.BlockSpec(memory_space=pl.ANY),
        scratch_shapes=[pltpu.VMEM((2, cap, d), F32), pltpu.SemaphoreType.DMA((2,)),
                        pltpu.SemaphoreType.DMA((2,))],
    )
    return pl.pallas_call(
        functools.partial(_combine_body, cap=cap, n_b=b),
        grid_spec=grid_spec,
        out_shape=jax.ShapeDtypeStruct((b, s, d), F32),
        input_output_aliases={1: 0},
        compiler_params=_params("arbitrary", "arbitrary"),
        name="combine",
    )(idx_flat, h, y)


def _moe(h, g, w_router, w_gate, w_up, w_down, layer):
    b, s, d = h.shape
    ne = w_router.shape[1]
    cap = CAPACITY_FACTOR * s // ne
    h2d = h.reshape(b * s, d)
    aff = _router(h2d, g, w_router, b, s)
    pos = _select(aff, cap)
    idx, gates = _compact(pos, aff, cap)
    idx_flat = idx.reshape(b * ne * cap)
    y = _expert_ffn(idx_flat, h2d, g, gates, w_gate, w_up, w_down, layer, b, s, cap)
    return _combine(idx_flat, h, y, cap)


def _alibi_slopes(n):
    return 2.0 ** (-8.0 * jnp.arange(1, n + 1, dtype=F32) / n)


def _mm_tiles(m, n):
    tm = next(t for t in (1024, 512, 256, 128, 8) if m % t == 0)
    tn = next(t for t in (512, 256, 128) if n % t == 0)
    return tm, tn


def _even_mixer(h, g, w_in, w_out, sink, qnorm, knorm):
    b, s, d = h.shape
    t = b * s
    n_heads = d // HEAD_DIM
    ha, hb = n_heads // 2, n_heads // 2
    kva, kvb = ha // 4, hb // 4
    q_a, kv_a, q_b, kv_b = ha * HEAD_DIM, kva * HEAD_DIM, hb * HEAD_DIM, kvb * HEAD_DIM
    h2d = h.reshape(t, d)
    hn = _rmsnorm(h2d, g, BF16)
    w_total = w_in.shape[1]
    tm, tn = _mm_tiles(t, w_total)
    col_scale = jnp.where(jnp.arange(w_total) < q_a, HEAD_DIM ** -0.5 * LOG2E, 1.0).astype(F32)
    proj = _matmul([hn], w_in, None, BF16, tm, tn, "in_proj_even", col_scale).reshape(b, s, w_total)
    out_a = _window_attention(proj, _alibi_slopes(ha), sink, n_heads=ha, n_kv=kva,
                              q_col=0, k_col=q_a, v_col=q_a + kv_a)
    qb0 = q_a + 2 * kv_a
    qk = _qk_prep(proj, jnp.stack([qnorm, knorm]), q_col=qb0, n_q_heads=hb, n_k_heads=kvb)
    out_b = _grid_attention(qk, proj, n_heads=hb, n_kv=kvb, v_col=qb0 + q_b + kv_b)
    tm, tn = _mm_tiles(t, d)
    out = _matmul([out_a.reshape(t, q_a), out_b.reshape(t, q_b)], w_out, h2d, F32, tm, tn, "out_proj_even")
    return out.reshape(b, s, d)


def _odd_mixer(h, g, w_in, w_out, c_lambda, c_subnorm, lambda_init):
    b, s, d = h.shape
    t = b * s
    n_heads = d // (2 * DIFF_DIM)
    h2d = h.reshape(t, d)
    hn = _rmsnorm(h2d, g, BF16)
    w_total = w_in.shape[1]
    tm, tn = _mm_tiles(t, w_total)
    col_scale = jnp.where(jnp.arange(w_total) < d, DIFF_DIM ** -0.5 * LOG2E, 1.0).astype(F32)
    proj = _matmul([hn], w_in, None, BF16, tm, tn, "in_proj_odd", col_scale).reshape(b, s, w_total)
    mix = _diff_attention(proj, _alibi_slopes(n_heads), c_lambda, c_subnorm, lambda_init, n_heads=n_heads)
    tm, tn = _mm_tiles(t, d)
    out = _matmul([mix.reshape(t, d)], w_out, h2d, F32, tm, tn, "out_proj_odd")
    return out.reshape(b, s, d)


def kernel(x, norm_mix, norm_ffn, norm_final, w_in_even, w_out_even, sink_a, qnorm_b, knorm_b,
           w_in_odd, w_out_odd, c_lambda, c_subnorm, w_router, w_gate, w_up, w_down):
    b, s, d = x.shape
    depth = norm_mix.shape[0]
    h = x
    for layer in range(depth):
        i = layer // 2
        if layer % 2 == 0:
            h = _even_mixer(h, norm_mix[layer], w_in_even[i], w_out_even[i], sink_a[i], qnorm_b[i], knorm_b[i])
        else:
            lambda_init = 0.8 - 0.6 * math.exp(-0.3 * layer)
            h = _odd_mixer(h, norm_mix[layer], w_in_odd[i], w_out_odd[i], c_lambda[i], c_subnorm[i], lambda_init)
        h = _moe(h, norm_ffn[layer], w_router[layer], w_gate, w_up, w_down, layer)
    return _rmsnorm(h.reshape(b * s, d), norm_final, F32).reshape(b, s, d)
```

```python
import functools
import math

import jax
import jax.numpy as jnp
from jax import lax
from jax.experimental import pallas as pl
from jax.experimental.pallas import tpu as pltpu

F32 = jnp.float32
BF16 = jnp.bfloat16

HEAD_DIM = 128
WINDOW = 128
BLOCK = 128
GRID_W = 64
ROPE_THETA = 10000.0
DIFF_DIM = 128
N_EXPERTS = 16
CAPACITY_FACTOR = 2
EPS = 1e-6
NEG = -1e30
LOG2E = math.log2(math.e)

VMEM_LIMIT_BYTES = 56 * 1024 * 1024

_NT = (((1,), (1,)), ((), ()))


def _params(*sem):
    return pltpu.CompilerParams(dimension_semantics=sem, vmem_limit_bytes=VMEM_LIMIT_BYTES)


def _rmsnorm_body(x_ref, g_ref, o_ref):
    x = x_ref[...]
    ms = jnp.mean(x * x, axis=-1, keepdims=True)
    o_ref[...] = (x * lax.rsqrt(ms + EPS) * g_ref[...]).astype(o_ref.dtype)


def _rmsnorm(x2d, g, out_dtype, tm=256):
    t, d = x2d.shape
    return pl.pallas_call(
        _rmsnorm_body,
        grid=(t // tm,),
        in_specs=[pl.BlockSpec((tm, d), lambda i: (i, 0)),
                  pl.BlockSpec((1, d), lambda i: (0, 0))],
        out_specs=pl.BlockSpec((tm, d), lambda i: (i, 0)),
        out_shape=jax.ShapeDtypeStruct((t, d), out_dtype),
        compiler_params=_params("parallel"),
        name="rmsnorm",
    )(x2d, g.reshape(1, d))


def _mm_body(*refs, n_pairs, has_res, has_scale):
    n_in = 2 * n_pairs + int(has_res) + int(has_scale)
    o_ref = refs[n_in]
    wbf = refs[n_in + 1:]

    @pl.when(pl.program_id(1) == 0)
    def _cast():
        for p in range(n_pairs):
            wbf[p][...] = refs[2 * p + 1][...].astype(BF16)

    acc = None
    for p in range(n_pairs):
        d = jnp.dot(refs[2 * p][...], wbf[p][...], preferred_element_type=F32)
        acc = d if acc is None else acc + d
    if has_res:
        acc = acc + refs[2 * n_pairs][...]
    if has_scale:
        acc = acc * refs[n_in - 1][...]
    o_ref[...] = acc.astype(o_ref.dtype)


def _matmul(a_list, w, res, out_dtype, tm, tn, name, col_scale=None):
    m = a_list[0].shape[0]
    n = w.shape[1]
    in_specs, args = [], []
    kp = a_list[0].shape[1]
    assert all(a.shape == (m, kp) for a in a_list) and w.shape[0] == kp * len(a_list)
    for part, a in enumerate(a_list):
        in_specs.append(pl.BlockSpec((tm, kp), lambda j, i: (i, 0)))
        in_specs.append(pl.BlockSpec((kp, tn), functools.partial(lambda j, i, r: (r, j), r=part)))
        args += [a, w]
    if res is not None:
        in_specs.append(pl.BlockSpec((tm, tn), lambda j, i: (i, j)))
        args.append(res)
    if col_scale is not None:
        in_specs.append(pl.BlockSpec((1, tn), lambda j, i: (0, j)))
        args.append(col_scale.reshape(1, n))
    return pl.pallas_call(
        functools.partial(_mm_body, n_pairs=len(a_list), has_res=res is not None,
                          has_scale=col_scale is not None),
        grid=(n // tn, m // tm),
        in_specs=in_specs,
        out_specs=pl.BlockSpec((tm, tn), lambda j, i: (i, j)),
        out_shape=jax.ShapeDtypeStruct((m, n), out_dtype),
        scratch_shapes=[pltpu.VMEM((kp, tn), BF16) for _ in a_list],
        compiler_params=_params("arbitrary", "arbitrary"),
        name=name,
    )(*args)


def _win_body(slope_ref, sink_ref, q_ref, kp_ref, kc_ref, kn_ref, vp_ref, vc_ref, vn_ref, o_ref,
              *, nb, group):
    kv = pl.program_id(1)
    pair = pl.program_id(2)
    tq, band = 2 * BLOCK, 4 * BLOCK
    kb = jnp.concatenate([kp_ref[0], kc_ref[0], kn_ref[0]], axis=0)
    vb = jnp.concatenate([vp_ref[0], vc_ref[0], vn_ref[0]], axis=0)
    kj = lax.broadcasted_iota(jnp.int32, (band, tq), 0)
    qi = lax.broadcasted_iota(jnp.int32, (band, tq), 1)
    arel = jnp.abs(BLOCK + qi - kj)
    kpos = (2 * pair - 1) * BLOCK + kj
    valid = (arel <= WINDOW) & (kpos >= 0) & (kpos < nb * BLOCK)
    arel_f = arel.astype(F32)
    heads = [kv * group + g for g in range(group)]
    q = q_ref[0]
    for g, h in enumerate(heads):
        bias = jnp.where(valid, -(slope_ref[h] * LOG2E) * arel_f, NEG)
        sk = sink_ref[h] * LOG2E
        s = lax.dot_general(kb, q[:, g * HEAD_DIM:(g + 1) * HEAD_DIM], _NT,
                            preferred_element_type=F32) + bias
        m = jnp.maximum(jnp.max(s, axis=0, keepdims=True), sk)
        p = jnp.exp2(s - m)
        den = jnp.sum(p, axis=0, keepdims=True) + jnp.exp2(sk - m)
        ot = lax.dot_general(vb, p.astype(BF16), (((0,), (0,)), ((), ())),
                             preferred_element_type=F32) / den
        o_ref[0, :, g * HEAD_DIM:(g + 1) * HEAD_DIM] = ot.T.astype(o_ref.dtype)


def _window_attention(proj, slopes, sink, *, n_heads, n_kv, q_col, k_col, v_col):
    b, s, _ = proj.shape
    nb = s // BLOCK
    group = n_heads // n_kv
    qw = group * HEAD_DIM
    qb0, kb0, vb0 = q_col // qw, k_col // HEAD_DIM, v_col // HEAD_DIM
    assert q_col % qw == 0 and k_col % HEAD_DIM == 0 and v_col % HEAD_DIM == 0

    assert nb % 2 == 0

    def edge_spec(col0, shift):
        def imap(bi, kv, pair):
            return (bi, jnp.clip(2 * pair + shift, 0, nb - 1), col0 + kv)
        return pl.BlockSpec((1, BLOCK, HEAD_DIM), imap)

    def pair_spec(col0):
        return pl.BlockSpec((1, 2 * BLOCK, HEAD_DIM), lambda bi, kv, pair: (bi, pair, col0 + kv))

    smem = pl.BlockSpec(memory_space=pltpu.SMEM)
    return pl.pallas_call(
        functools.partial(_win_body, nb=nb, group=group),
        grid=(b, n_kv, nb // 2),
        in_specs=[smem, smem,
                  pl.BlockSpec((1, 2 * BLOCK, qw), lambda bi, kv, pair: (bi, pair, qb0 + kv)),
                  edge_spec(kb0, -1), pair_spec(kb0), edge_spec(kb0, 2),
                  edge_spec(vb0, -1), pair_spec(vb0), edge_spec(vb0, 2)],
        out_specs=pl.BlockSpec((1, 2 * BLOCK, qw), lambda bi, kv, pair: (bi, pair, kv)),
        out_shape=jax.ShapeDtypeStruct((b, s, n_heads * HEAD_DIM), BF16),
        compiler_params=_params("parallel", "parallel", "parallel"),
        name="window_attn",
    )(slopes, sink, proj, proj, proj, proj, proj, proj, proj)


def _qkprep_body(x_ref, g_ref, cos_ref, sin_ref, o_ref, *, n_qchunks, heads_per_chunk, scale):
    j = pl.program_id(2)
    is_k = j >= n_qchunks
    gain = jnp.where(is_k, g_ref[1:2, :], g_ref[0:1, :])
    mult = jnp.where(is_k, 1.0, scale)
    cos = cos_ref[...]
    sin = sin_ref[...]
    lane = lax.broadcasted_iota(jnp.int32, cos.shape, 1)
    first = (lane % (HEAD_DIM // 2)) < (HEAD_DIM // 4)
    for hd in range(heads_per_chunk):
        x = x_ref[0, :, hd * HEAD_DIM:(hd + 1) * HEAD_DIM].astype(F32)
        ms = jnp.mean(x * x, axis=-1, keepdims=True)
        y = x * lax.rsqrt(ms + EPS) * gain
        partner = jnp.where(first, pltpu.roll(y, HEAD_DIM - HEAD_DIM // 4, 1),
                            pltpu.roll(y, HEAD_DIM // 4, 1))
        r = y * cos + partner * sin
        o_ref[0, :, hd * HEAD_DIM:(hd + 1) * HEAD_DIM] = (r * mult).astype(o_ref.dtype)


def _rope_tables(s):
    rows = s // GRID_W
    row = jnp.repeat(jnp.arange(rows), GRID_W).astype(F32)
    col = jnp.tile(jnp.arange(GRID_W), rows).astype(F32)
    quarter = HEAD_DIM // 4
    inv = ROPE_THETA ** (-jnp.arange(quarter, dtype=F32) / quarter)
    ang_r = row[:, None] * inv[None, :]
    ang_c = col[:, None] * inv[None, :]
    cos = jnp.concatenate([jnp.cos(ang_r)] * 2 + [jnp.cos(ang_c)] * 2, axis=-1)
    sin = jnp.concatenate([-jnp.sin(ang_r), jnp.sin(ang_r), -jnp.sin(ang_c), jnp.sin(ang_c)], axis=-1)
    return cos, sin


def _qk_prep(proj, gains, *, q_col, n_q_heads, n_k_heads, ts=256):
    b, s, _ = proj.shape
    cw = n_k_heads * HEAD_DIM
    assert q_col % cw == 0 and (n_q_heads * HEAD_DIM) % cw == 0
    n_qchunks = n_q_heads * HEAD_DIM // cw
    c0 = q_col // cw
    cos, sin = _rope_tables(s)
    return pl.pallas_call(
        functools.partial(_qkprep_body, n_qchunks=n_qchunks, heads_per_chunk=n_k_heads,
                          scale=HEAD_DIM ** -0.5 * LOG2E),
        grid=(b, s // ts, n_qchunks + 1),
        in_specs=[pl.BlockSpec((1, ts, cw), lambda bi, i, j: (bi, i, c0 + j)),
                  pl.BlockSpec((2, HEAD_DIM), lambda bi, i, j: (0, 0)),
                  pl.BlockSpec((ts, HEAD_DIM), lambda bi, i, j: (i, 0)),
                  pl.BlockSpec((ts, HEAD_DIM), lambda bi, i, j: (i, 0))],
        out_specs=pl.BlockSpec((1, ts, cw), lambda bi, i, j: (bi, i, j)),
        out_shape=jax.ShapeDtypeStruct((b, s, (n_qchunks + 1) * cw), BF16),
        compiler_params=_params("parallel", "parallel", "parallel"),
        name="qk_prep",
    )(proj, gains, cos, sin)


def _lane_groups(x):
    return [x[:, g * 128:(g + 1) * 128] for g in range(x.shape[1] // 128)]


def _grid_attn_body(q_ref, k_ref, v_ref, o_ref, *, tq, group, kc, s_len, unroll):
    q = q_ref[0]
    qs = jnp.concatenate([q[:, g * HEAD_DIM:(g + 1) * HEAD_DIM] for g in range(group)], axis=0)
    rows = group * tq

    def step(j, carry):
        m, lp, acc = carry
        off = pl.multiple_of(j * kc, kc)
        sg = _lane_groups(lax.dot_general(qs, k_ref[0, pl.ds(off, kc), :], _NT, preferred_element_type=F32))
        cmax = jnp.max(functools.reduce(jnp.maximum, sg), axis=-1, keepdims=True)
        m_new = jnp.maximum(m, jnp.broadcast_to(cmax, (rows, 128)))
        alpha = jnp.exp2(m - m_new)
        ps = [jnp.exp2(x - m_new) for x in sg]
        lp = alpha * lp + functools.reduce(jnp.add, ps)
        p = jnp.concatenate([pg.astype(BF16) for pg in ps], axis=1)
        acc = alpha * acc + jnp.dot(p, v_ref[0, pl.ds(off, kc), :], preferred_element_type=F32)
        return m_new, lp, acc

    assert HEAD_DIM == 128
    init = (jnp.full((rows, 128), NEG, F32), jnp.zeros((rows, 128), F32), jnp.zeros((rows, HEAD_DIM), F32))
    _, lp, acc = lax.fori_loop(0, s_len // kc, step, init, unroll=unroll)
    o = acc / jnp.sum(lp, axis=-1, keepdims=True)
    for g in range(group):
        o_ref[0, :, g * HEAD_DIM:(g + 1) * HEAD_DIM] = o[g * tq:(g + 1) * tq].astype(o_ref.dtype)


def _grid_attention(qk, proj, *, n_heads, n_kv, v_col, tq=256, kc=512, unroll=2):
    b, s, _ = qk.shape
    group = n_heads // n_kv
    qw = group * HEAD_DIM
    kc = min(kc, s)
    vb0 = v_col // HEAD_DIM
    return pl.pallas_call(
        functools.partial(_grid_attn_body, tq=tq, group=group, kc=kc, s_len=s, unroll=unroll),
        grid=(b, n_kv, s // tq),
        in_specs=[pl.BlockSpec((1, tq, qw), lambda bi, kv, i: (bi, i, kv)),
                  pl.BlockSpec((1, s, HEAD_DIM), lambda bi, kv, i: (bi, 0, n_heads + kv)),
                  pl.BlockSpec((1, s, HEAD_DIM), lambda bi, kv, i: (bi, 0, vb0 + kv))],
        out_specs=pl.BlockSpec((1, tq, qw), lambda bi, kv, i: (bi, i, kv)),
        out_shape=jax.ShapeDtypeStruct((b, s, n_heads * HEAD_DIM), BF16),
        compiler_params=_params("parallel", "parallel", "arbitrary"),
        name="grid_attn",
    )(qk, qk, proj)


def _diff_attn_body(slope_ref, lamc_ref, sub_ref, q_ref, k_ref, v_ref, o_ref,
                    *, tq, kc, s_len, lambda_init, unroll):
    h = pl.program_id(1)
    r = pl.program_id(2)
    slope2 = slope_ref[h] * LOG2E
    q = q_ref[0]
    zq = jnp.zeros((tq, DIFF_DIM), q.dtype)
    qd = jnp.concatenate([jnp.concatenate([q[:, :DIFF_DIM], zq], axis=1),
                          jnp.concatenate([zq, q[:, DIFF_DIM:]], axis=1)], axis=0)
    rows = 2 * tq
    n_chunks = s_len // kc
    i0 = r * tq
    c_diag = i0 // kc
    kj = lax.broadcasted_iota(jnp.int32, (kc, 128), 0).astype(F32)
    qi = (lax.broadcasted_iota(jnp.int32, (1, rows), 1) % tq).astype(F32)

    def off_diag(c):
        return c + (c >= c_diag).astype(jnp.int32)

    def terms(cc):
        is_left = cc < c_diag
        gap = jnp.where(is_left, i0 - (cc + 1) * kc, cc * kc - i0 - tq).astype(F32)
        keyterm = -slope2 * jnp.where(is_left, kc - kj, kj)
        qterm = -slope2 * (jnp.where(is_left, qi, tq - qi) + gap)
        return [keyterm] * (rows // 128), qterm

    def diag_bias():
        in_chunk = (i0 - c_diag * kc).astype(F32)
        kfull = lax.broadcasted_iota(jnp.int32, (kc, rows), 0).astype(F32)
        return _lane_groups(-slope2 * jnp.abs(qi + in_chunk - kfull))

    def probs(cc, bias_groups, qterm, m, l):
        off = pl.multiple_of(cc * kc, kc)
        st = lax.dot_general(k_ref[0, pl.ds(off, kc), :], qd, _NT, preferred_element_type=F32)
        ug = [x + bg for x, bg in zip(_lane_groups(st), bias_groups)]
        cmax = jnp.concatenate([jnp.max(x, axis=0, keepdims=True) for x in ug], axis=1)
        m_new = jnp.maximum(m, cmax + qterm)
        alpha = jnp.exp2(m - m_new)
        shift = m_new - qterm
        ps = [jnp.exp2(x - shift[:, g * 128:(g + 1) * 128]) for g, x in enumerate(ug)]
        l = alpha * l + jnp.concatenate([jnp.sum(x, axis=0, keepdims=True) for x in ps], axis=1)
        pt = jnp.concatenate([x.astype(BF16) for x in ps], axis=1)
        return m_new, l, alpha, pt

    def weighted_values(cc, alpha, pt, acc):
        off = pl.multiple_of(cc * kc, kc)
        return alpha * acc + lax.dot_general(v_ref[0, pl.ds(off, kc), :], pt, (((0,), (0,)), ((), ())),
                                             preferred_element_type=F32)

    def step(c, carry):
        m, l, acc, alpha, pt, prev = carry
        cc = off_diag(c)
        bias_groups, qterm = terms(cc)
        m, l, alpha_next, pt_next = probs(cc, bias_groups, qterm, m, l)
        acc = weighted_values(prev, alpha, pt, acc)
        return m, l, acc, alpha_next, pt_next, cc

    m0 = jnp.full((1, rows), NEG, F32)
    m, l, alpha, pt = probs(c_diag, diag_bias(), 0.0, m0, jnp.zeros((1, rows), F32))
    carry = (m, l, jnp.zeros((2 * DIFF_DIM, rows), F32), alpha, pt, c_diag)
    _, l, acc, alpha, pt, last = lax.fori_loop(0, n_chunks - 1, step, carry, unroll=unroll)
    ot = weighted_values(last, alpha, pt, acc) / l
    lf = lamc_ref[...]
    lam = (jnp.exp(jnp.sum(lf[0:1] * lf[1:2], axis=-1, keepdims=True))
           - jnp.exp(jnp.sum(lf[2:3] * lf[3:4], axis=-1, keepdims=True)) + lambda_init)
    dt = ot[:, :tq] - lam * ot[:, tq:]
    ms = jnp.mean(dt * dt, axis=0, keepdims=True)
    o = (dt * lax.rsqrt(ms + EPS)).T * sub_ref[...] * (1.0 - lambda_init)
    o_ref[0] = o.astype(o_ref.dtype)


def _diff_attention(proj, slopes, c_lambda, c_subnorm, lambda_init, *, n_heads, tq=256, kc=512, unroll=2):
    b, s, _ = proj.shape
    hw = 2 * DIFF_DIM
    kc = min(kc, s)
    tq = min(tq, kc)
    assert kc % tq == 0 and s % kc == 0
    return pl.pallas_call(
        functools.partial(_diff_attn_body, tq=tq, kc=kc, s_len=s, lambda_init=lambda_init, unroll=unroll),
        grid=(b, n_heads, s // tq),
        in_specs=[pl.BlockSpec(memory_space=pltpu.SMEM),
                  pl.BlockSpec((4, DIFF_DIM), lambda bi, h, i: (0, 0)),
                  pl.BlockSpec((1, hw), lambda bi, h, i: (0, 0)),
                  pl.BlockSpec((1, tq, hw), lambda bi, h, i: (bi, i, h)),
                  pl.BlockSpec((1, s, hw), lambda bi, h, i: (bi, 0, n_heads + h)),
                  pl.BlockSpec((1, s, hw), lambda bi, h, i: (bi, 0, 2 * n_heads + h))],
        out_specs=pl.BlockSpec((1, tq, hw), lambda bi, h, i: (bi, i, h)),
        out_shape=jax.ShapeDtypeStruct((b, s, n_heads * hw), BF16),
        compiler_params=_params("parallel", "parallel", "arbitrary"),
        name="diff_attn",
    )(slopes, c_lambda, c_subnorm.reshape(1, hw), proj, proj, proj)


def _router_body(h_ref, g_ref, wrt_ref, aff_ref):
    x = h_ref[...]
    ms = jnp.mean(x * x, axis=-1, keepdims=True)
    xn = x * lax.rsqrt(ms + EPS) * g_ref[...]
    logits = lax.dot_general(wrt_ref[...], xn, _NT, precision=lax.Precision.HIGHEST,
                             preferred_element_type=F32)
    m = jnp.max(logits, axis=0, keepdims=True)
    e = jnp.exp(logits - m)
    aff_ref[0] = e / jnp.sum(e, axis=0, keepdims=True)


def _router(h2d, g, w_router, b, s, tm=512):
    t, d = h2d.shape
    tm = min(tm, s)
    spb = s // tm
    ne = w_router.shape[1]
    return pl.pallas_call(
        _router_body,
        grid=(t // tm,),
        in_specs=[pl.BlockSpec((tm, d), lambda i: (i, 0)),
                  pl.BlockSpec((1, d), lambda i: (0, 0)),
                  pl.BlockSpec((ne, d), lambda i: (0, 0))],
        out_specs=pl.BlockSpec((1, ne, tm), lambda i: (i // spb, 0, i % spb)),
        out_shape=jax.ShapeDtypeStruct((b, ne, s), F32),
        compiler_params=_params("parallel"),
        name="router",
    )(h2d, g.reshape(1, d), w_router.T)


def _cumsum_lanes(mask, tri):
    rows, n = mask.shape
    carry = jnp.zeros((rows, 1), F32)
    out = []
    for blk in range(n // 128):
        c = jnp.dot(mask[:, blk * 128:(blk + 1) * 128].astype(BF16), tri, preferred_element_type=F32) + carry
        out.append(c)
        carry = c[:, 127:128]
    return jnp.concatenate(out, axis=1)


def _select_body(aff_ref, pos_ref, *, cap):
    a = aff_ref[0]
    bits = pltpu.bitcast(a, jnp.int32)
    ne = a.shape[0]
    thr = jnp.zeros((ne, 1), jnp.int32)
    for bit in range(30, -1, -1):
        cand = thr | (1 << bit)
        cnt = jnp.sum((bits >= cand).astype(jnp.int32), axis=1, keepdims=True)
        thr = jnp.where(cnt >= cap, cand, thr)
    gt = bits > thr
    eq = bits == thr
    need = (cap - jnp.sum(gt.astype(jnp.int32), axis=1, keepdims=True)).astype(F32)
    r = lax.broadcasted_iota(jnp.int32, (128, 128), 0)
    c = lax.broadcasted_iota(jnp.int32, (128, 128), 1)
    tri = (r <= c).astype(BF16)
    sel = gt | (eq & (_cumsum_lanes(eq.astype(F32), tri) <= need))
    pos = _cumsum_lanes(sel.astype(F32), tri) - 1.0
    pos_ref[0] = jnp.where(sel, pos, -1.0)


def _select(aff, cap):
    b, ne, s = aff.shape
    return pl.pallas_call(
        functools.partial(_select_body, cap=cap),
        grid=(b,),
        in_specs=[pl.BlockSpec((1, ne, s), lambda bi: (bi, 0, 0))],
        out_specs=pl.BlockSpec((1, ne, s), lambda bi: (bi, 0, 0)),
        out_shape=jax.ShapeDtypeStruct((b, ne, s), F32),
        compiler_params=_params("parallel"),
        name="select",
    )(aff)


def _compact_body(pos_ref, aff_ref, idx_ref, gate_ref, *, tc):
    c0 = pl.program_id(1) * tc
    pos = pos_ref[0]
    s = pos.shape[1]
    slot = (c0 + lax.broadcasted_iota(jnp.int32, (tc, s), 0)).astype(F32)
    tok = lax.broadcasted_iota(jnp.int32, (tc, s), 1)
    hit = pos == slot
    idx_ref[0] = jnp.sum(jnp.where(hit, tok, 0), axis=1, keepdims=True)
    gate_ref[0] = jnp.sum(jnp.where(hit, aff_ref[0], 0.0), axis=1, keepdims=True)


def _compact(pos, aff, cap, tc=128):
    b, ne, s = pos.shape
    tc = min(tc, cap)
    row = pl.BlockSpec((1, 1, s), lambda r, j: (r, 0, 0))
    col = pl.BlockSpec((1, tc, 1), lambda r, j: (r, j, 0))
    return pl.pallas_call(
        functools.partial(_compact_body, tc=tc),
        grid=(b * ne, cap // tc),
        in_specs=[row, row],
        out_specs=[col, col],
        out_shape=[jax.ShapeDtypeStruct((b * ne, cap, 1), jnp.int32),
                   jax.ShapeDtypeStruct((b * ne, cap, 1), F32)],
        compiler_params=_params("parallel", "parallel"),
        name="compact",
    )(pos.reshape(b * ne, 1, s), aff.reshape(b * ne, 1, s))


def _ffn_body(idx_ref, h_hbm, g_ref, gate_ref, wg_ref, wu_ref, wd_ref, o_ref, land, xg, hid, sems,
              *, cap, s_len, n_b, n_f, piece):
    e = pl.program_id(0)
    step = pl.program_id(1)
    ne = pl.num_programs(0)
    rows = n_b * cap
    n_pieces = rows // piece

    def row_copy(src_row, slot, r):
        return pltpu.make_async_copy(h_hbm.at[pl.ds(src_row, 1)], land.at[slot, pl.ds(r, 1)], sems.at[slot])

    def issue(p):
        bi, c0 = divmod(p * piece, cap)
        base = (bi * ne + e) * cap + c0

        def body(r, carry):
            row_copy(bi * s_len + idx_ref[base + r], p % 2, r).start()
            return carry

        lax.fori_loop(0, piece, body, 0, unroll=8)

    def drain(p):
        def body(r, carry):
            row_copy(0, p % 2, r).wait()
            return carry

        lax.fori_loop(0, piece, body, 0, unroll=8)

    @pl.when(step == 0)
    def _gather_and_norm():
        issue(0)
        for p in range(n_pieces):
            if p + 1 < n_pieces:
                issue(p + 1)
            drain(p)
            x = land[p % 2]
            ms = jnp.mean(x * x, axis=-1, keepdims=True)
            xg[p * piece:(p + 1) * piece, :] = (x * lax.rsqrt(ms + EPS) * g_ref[...]).astype(BF16)

    @pl.when(step < n_f)
    def _up():
        x = xg[...]
        gt = jnp.dot(x, wg_ref[0, 0].astype(BF16), preferred_element_type=F32)
        up = jnp.dot(x, wu_ref[0, 0].astype(BF16), preferred_element_type=F32)
        hid[step] = (gt * jax.nn.sigmoid(gt) * up).astype(BF16)

    @pl.when(step >= n_f)
    def _down():
        hcat = jnp.concatenate([hid[f] for f in range(n_f)], axis=1)
        y = jnp.dot(hcat, wd_ref[0, 0].astype(BF16), preferred_element_type=F32)
        o_ref[0] = y * gate_ref[:, 0].reshape(rows, 1)


def _expert_ffn(idx_flat, h2d, g, gates, w_gate, w_up, w_down, layer, b, s, cap, tf=256, tn=512, piece=128):
    t, d = h2d.shape
    _, ne, _, dff = w_gate.shape
    tf = min(tf, dff)
    tn = min(tn, d)
    n_f = dff // tf
    n_d = d // tn
    rows = b * cap
    piece = min(piece, cap)
    grid_spec = pltpu.PrefetchScalarGridSpec(
        num_scalar_prefetch=1,
        grid=(ne, n_f + n_d),
        in_specs=[pl.BlockSpec(memory_space=pl.ANY),
                  pl.BlockSpec((1, d), lambda e, st, idx: (0, 0)),
                  pl.BlockSpec((b, 1, cap, 1), lambda e, st, idx: (0, e, 0, 0)),
                  pl.BlockSpec((1, 1, d, tf), lambda e, st, idx: (layer, e, 0, jnp.minimum(st, n_f - 1))),
                  pl.BlockSpec((1, 1, d, tf), lambda e, st, idx: (layer, e, 0, jnp.minimum(st, n_f - 1))),
                  pl.BlockSpec((1, 1, dff, tn), lambda e, st, idx: (layer, e, 0, jnp.maximum(st - n_f, 0)))],
        out_specs=pl.BlockSpec((1, rows, tn), lambda e, st, idx: (e, 0, jnp.maximum(st - n_f, 0))),
        scratch_shapes=[pltpu.VMEM((2, piece, d), F32), pltpu.VMEM((rows, d), BF16),
                        pltpu.VMEM((n_f, rows, tf), BF16), pltpu.SemaphoreType.DMA((2,))],
    )
    return pl.pallas_call(
        functools.partial(_ffn_body, cap=cap, s_len=s, n_b=b, n_f=n_f, piece=piece),
        grid_spec=grid_spec,
        out_shape=jax.ShapeDtypeStruct((ne, rows, d), F32),
        compiler_params=_params("arbitrary", "arbitrary"),
        name="expert_ffn",
    )(idx_flat, h2d, g.reshape(1, d), gates.reshape(b, ne, cap, 1), w_gate, w_up, w_down)


def _combine_body(idx_ref, h_hbm, y_ref, o_hbm, buf, gsem, ssem, *, cap, n_b):
    del h_hbm
    e = pl.program_id(0)
    b = pl.program_id(1)
    ne = pl.num_programs(0)
    n_steps = ne * n_b
    step = e * n_b + b
    slot = step % 2

    def gather_row(c, bi, tok, sl):
        return pltpu.make_async_copy(o_hbm.at[bi, pl.ds(tok, 1)], buf.at[sl, pl.ds(c, 1)], gsem.at[sl])

    def scatter_row(c, tok, sl):
        return pltpu.make_async_copy(buf.at[sl, pl.ds(c, 1)], o_hbm.at[b, pl.ds(tok, 1)], ssem.at[sl])

    def for_rows(fn):
        def body(c, carry):
            fn(c)
            return carry
        lax.fori_loop(0, cap, body, 0, unroll=8)

    def start_gather(st, sl):
        e2, b2 = st // n_b, st % n_b
        base2 = (b2 * ne + e2) * cap
        for_rows(lambda c: gather_row(c, b2, idx_ref[base2 + c], sl).start())

    def wait_scatter(sl):
        for_rows(lambda c: scatter_row(c, 0, sl).wait())

    pl.when(step == 0)(lambda: start_gather(0, 0))
    for_rows(lambda c: gather_row(c, b, 0, slot).wait())
    buf[slot] = buf[slot] + y_ref[0]
    base = (b * ne + e) * cap
    for_rows(lambda c: scatter_row(c, idx_ref[base + c], slot).start())
    if n_b == 1:
        wait_scatter(slot)
    else:
        pl.when(step > 0)(lambda: wait_scatter(1 - slot))
    pl.when(step + 1 < n_steps)(lambda: start_gather(step + 1, 1 - slot))
    if n_b > 1:
        pl.when(step == n_steps - 1)(lambda: wait_scatter(slot))


def _combine(idx_flat, h, y, cap):
    b, s, d = h.shape
    ne = y.shape[0]
    grid_spec = pltpu.PrefetchScalarGridSpec(
        num_scalar_prefetch=1,
        grid=(ne, b),
        in_specs=[pl.BlockSpec(memory_space=pl.ANY),
                  pl.BlockSpec((1, cap, d), lambda e, bi, idx: (e, bi, 0))],
        out_specs=pl.BlockSpec(memory_space=pl.ANY),
        scratch_shapes=[pltpu.VMEM((2, cap, d), F32), pltpu.SemaphoreType.DMA((2,)),
                        pltpu.SemaphoreType.DMA((2,))],
    )
    return pl.pallas_call(
        functools.partial(_combine_body, cap=cap, n_b=b),
        grid_spec=grid_spec,
        out_shape=jax.ShapeDtypeStruct((b, s, d), F32),
        input_output_aliases={1: 0},
        compiler_params=_params("arbitrary", "arbitrary"),
        name="combine",
    )(idx_flat, h, y)


def _moe(h, g, w_router, w_gate, w_up, w_down, layer):
    b, s, d = h.shape
    ne = w_router.shape[1]
    cap = CAPACITY_FACTOR * s // ne
    h2d = h.reshape(b * s, d)
    aff = _router(h2d, g, w_router, b, s)
    pos = _select(aff, cap)
    idx, gates = _compact(pos, aff, cap)
    idx_flat = idx.reshape(b * ne * cap)
    y = _expert_ffn(idx_flat, h2d, g, gates, w_gate, w_up, w_down, layer, b, s, cap)
    return _combine(idx_flat, h, y, cap)


def _alibi_slopes(n):
    return 2.0 ** (-8.0 * jnp.arange(1, n + 1, dtype=F32) / n)


def _mm_tiles(m, n):
    tm = next(t for t in (1024, 512, 256, 128, 8) if m % t == 0)
    tn = next(t for t in (512, 256, 128) if n % t == 0)
    return tm, tn


def _even_mixer(h, g, w_in, w_out, sink, qnorm, knorm):
    b, s, d = h.shape
    t = b * s
    n_heads = d // HEAD_DIM
    ha, hb = n_heads // 2, n_heads // 2
    kva, kvb = ha // 4, hb // 4
    q_a, kv_a, q_b, kv_b = ha * HEAD_DIM, kva * HEAD_DIM, hb * HEAD_DIM, kvb * HEAD_DIM
    h2d = h.reshape(t, d)
    hn = _rmsnorm(h2d, g, BF16)
    w_total = w_in.shape[1]
    tm, tn = _mm_tiles(t, w_total)
    col_scale = jnp.where(jnp.arange(w_total) < q_a, HEAD_DIM ** -0.5 * LOG2E, 1.0).astype(F32)
    proj = _matmul([hn], w_in, None, BF16, tm, tn, "in_proj_even", col_scale).reshape(b, s, w_total)
    out_a = _window_attention(proj, _alibi_slopes(ha), sink, n_heads=ha, n_kv=kva,
                              q_col=0, k_col=q_a, v_col=q_a + kv_a)
    qb0 = q_a + 2 * kv_a
    qk = _qk_prep(proj, jnp.stack([qnorm, knorm]), q_col=qb0, n_q_heads=hb, n_k_heads=kvb)
    out_b = _grid_attention(qk, proj, n_heads=hb, n_kv=kvb, v_col=qb0 + q_b + kv_b)
    tm, tn = _mm_tiles(t, d)
    out = _matmul([out_a.reshape(t, q_a), out_b.reshape(t, q_b)], w_out, h2d, F32, tm, tn, "out_proj_even")
    return out.reshape(b, s, d)


def _odd_mixer(h, g, w_in, w_out, c_lambda, c_subnorm, lambda_init):
    b, s, d = h.shape
    t = b * s
    n_heads = d // (2 * DIFF_DIM)
    h2d = h.reshape(t, d)
    hn = _rmsnorm(h2d, g, BF16)
    w_total = w_in.shape[1]
    tm, tn = _mm_tiles(t, w_total)
    col_scale = jnp.where(jnp.arange(w_total) < d, DIFF_DIM ** -0.5 * LOG2E, 1.0).astype(F32)
    proj = _matmul([hn], w_in, None, BF16, tm, tn, "in_proj_odd", col_scale).reshape(b, s, w_total)
    mix = _diff_attention(proj, _alibi_slopes(n_heads), c_lambda, c_subnorm, lambda_init, n_heads=n_heads)
    tm, tn = _mm_tiles(t, d)
    out = _matmul([mix.reshape(t, d)], w_out, h2d, F32, tm, tn, "out_proj_odd")
    return out.reshape(b, s, d)


def kernel(x, norm_mix, norm_ffn, norm_final, w_in_even, w_out_even, sink_a, qnorm_b, knorm_b,
           w_in_odd, w_out_odd, c_lambda, c_subnorm, w_router, w_gate, w_up, w_down):
    b, s, d = x.shape
    depth = norm_mix.shape[0]
    h = x
    for layer in range(depth):
        i = layer // 2
        if layer % 2 == 0:
            h = _even_mixer(h, norm_mix[layer], w_in_even[i], w_out_even[i], sink_a[i], qnorm_b[i], knorm_b[i])
        else:
            lambda_init = 0.8 - 0.6 * math.exp(-0.3 * layer)
            h = _odd_mixer(h, norm_mix[layer], w_in_odd[i], w_out_odd[i], c_lambda[i], c_subnorm[i], lambda_init)
        h = _moe(h, norm_ffn[layer], w_router[layer], w_gate, w_up, w_down, layer)
    return _rmsnorm(h.reshape(b * s, d), norm_final, F32).reshape(b, s, d)
```

```python
import functools
import math

import jax
import jax.numpy as jnp
from jax import lax
from jax.experimental import pallas as pl
from jax.experimental.pallas import tpu as pltpu

F32 = jnp.float32
BF16 = jnp.bfloat16

HEAD_DIM = 128
WINDOW = 128
BLOCK = 128
GRID_W = 64
ROPE_THETA = 10000.0
DIFF_DIM = 128
N_EXPERTS = 16
CAPACITY_FACTOR = 2
EPS = 1e-6
NEG = -1e30
LOG2E = math.log2(math.e)

VMEM_LIMIT_BYTES = 56 * 1024 * 1024

_NT = (((1,), (1,)), ((), ()))


def _params(*sem):
    return pltpu.CompilerParams(dimension_semantics=sem, vmem_limit_bytes=VMEM_LIMIT_BYTES)


def _rmsnorm_body(x_ref, g_ref, o_ref):
    x = x_ref[...]
    ms = jnp.mean(x * x, axis=-1, keepdims=True)
    o_ref[...] = (x * lax.rsqrt(ms + EPS) * g_ref[...]).astype(o_ref.dtype)


def _rmsnorm(x2d, g, out_dtype, tm=256):
    t, d = x2d.shape
    return pl.pallas_call(
        _rmsnorm_body,
        grid=(t // tm,),
        in_specs=[pl.BlockSpec((tm, d), lambda i: (i, 0)),
                  pl.BlockSpec((1, d), lambda i: (0, 0))],
        out_specs=pl.BlockSpec((tm, d), lambda i: (i, 0)),
        out_shape=jax.ShapeDtypeStruct((t, d), out_dtype),
        compiler_params=_params("parallel"),
        name="rmsnorm",
    )(x2d, g.reshape(1, d))


def _mm_body(*refs, n_pairs, has_res, has_scale):
    n_in = 2 * n_pairs + int(has_res) + int(has_scale)
    o_ref = refs[n_in]
    wbf = refs[n_in + 1:]

    @pl.when(pl.program_id(1) == 0)
    def _cast():
        for p in range(n_pairs):
            wbf[p][...] = refs[2 * p + 1][...].astype(BF16)

    acc = None
    for p in range(n_pairs):
        d = jnp.dot(refs[2 * p][...], wbf[p][...], preferred_element_type=F32)
        acc = d if acc is None else acc + d
    if has_res:
        acc = acc + refs[2 * n_pairs][...]
    if has_scale:
        acc = acc * refs[n_in - 1][...]
    o_ref[...] = acc.astype(o_ref.dtype)


def _matmul(a_list, w, res, out_dtype, tm, tn, name, col_scale=None):
    m = a_list[0].shape[0]
    n = w.shape[1]
    in_specs, args = [], []
    kp = a_list[0].shape[1]
    assert all(a.shape == (m, kp) for a in a_list) and w.shape[0] == kp * len(a_list)
    for part, a in enumerate(a_list):
        in_specs.append(pl.BlockSpec((tm, kp), lambda j, i: (i, 0)))
        in_specs.append(pl.BlockSpec((kp, tn), functools.partial(lambda j, i, r: (r, j), r=part)))
        args += [a, w]
    if res is not None:
        in_specs.append(pl.BlockSpec((tm, tn), lambda j, i: (i, j)))
        args.append(res)
    if col_scale is not None:
        in_specs.append(pl.BlockSpec((1, tn), lambda j, i: (0, j)))
        args.append(col_scale.reshape(1, n))
    return pl.pallas_call(
        functools.partial(_mm_body, n_pairs=len(a_list), has_res=res is not None,
                          has_scale=col_scale is not None),
        grid=(n // tn, m // tm),
        in_specs=in_specs,
        out_specs=pl.BlockSpec((tm, tn), lambda j, i: (i, j)),
        out_shape=jax.ShapeDtypeStruct((m, n), out_dtype),
        scratch_shapes=[pltpu.VMEM((kp, tn), BF16) for _ in a_list],
        compiler_params=_params("arbitrary", "arbitrary"),
        name=name,
    )(*args)


def _win_body(slope_ref, sink_ref, q_ref, kp_ref, kc_ref, kn_ref, vp_ref, vc_ref, vn_ref, o_ref,
              *, nb, group):
    kv = pl.program_id(1)
    pair = pl.program_id(2)
    tq, band = 2 * BLOCK, 4 * BLOCK
    kb = jnp.concatenate([kp_ref[0], kc_ref[0], kn_ref[0]], axis=0)
    vb = jnp.concatenate([vp_ref[0], vc_ref[0], vn_ref[0]], axis=0)
    kj = lax.broadcasted_iota(jnp.int32, (band, tq), 0)
    qi = lax.broadcasted_iota(jnp.int32, (band, tq), 1)
    arel = jnp.abs(BLOCK + qi - kj)
    kpos = (2 * pair - 1) * BLOCK + kj
    valid = (arel <= WINDOW) & (kpos >= 0) & (kpos < nb * BLOCK)
    arel_f = arel.astype(F32)
    heads = [kv * group + g for g in range(group)]
    q = q_ref[0]
    for g, h in enumerate(heads):
        bias = jnp.where(valid, -(slope_ref[h] * LOG2E) * arel_f, NEG)
        sk = sink_ref[h] * LOG2E
        s = lax.dot_general(kb, q[:, g * HEAD_DIM:(g + 1) * HEAD_DIM], _NT,
                            preferred_element_type=F32) + bias
        m = jnp.maximum(jnp.max(s, axis=0, keepdims=True), sk)
        p = jnp.exp2(s - m)
        den = jnp.sum(p, axis=0, keepdims=True) + jnp.exp2(sk - m)
        ot = lax.dot_general(vb, p.astype(BF16), (((0,), (0,)), ((), ())),
                             preferred_element_type=F32) / den
        o_ref[0, :, g * HEAD_DIM:(g + 1) * HEAD_DIM] = ot.T.astype(o_ref.dtype)


def _window_attention(proj, slopes, sink, *, n_heads, n_kv, q_col, k_col, v_col):
    b, s, _ = proj.shape
    nb = s // BLOCK
    group = n_heads // n_kv
    qw = group * HEAD_DIM
    qb0, kb0, vb0 = q_col // qw, k_col // HEAD_DIM, v_col // HEAD_DIM
    assert q_col % qw == 0 and k_col % HEAD_DIM == 0 and v_col % HEAD_DIM == 0

    assert nb % 2 == 0

    def edge_spec(col0, shift):
        def imap(bi, kv, pair):
            return (bi, jnp.clip(2 * pair + shift, 0, nb - 1), col0 + kv)
        return pl.BlockSpec((1, BLOCK, HEAD_DIM), imap)

    def pair_spec(col0):
        return pl.BlockSpec((1, 2 * BLOCK, HEAD_DIM), lambda bi, kv, pair: (bi, pair, col0 + kv))

    smem = pl.BlockSpec(memory_space=pltpu.SMEM)
    return pl.pallas_call(
        functools.partial(_win_body, nb=nb, group=group),
        grid=(b, n_kv, nb // 2),
        in_specs=[smem, smem,
                  pl.BlockSpec((1, 2 * BLOCK, qw), lambda bi, kv, pair: (bi, pair, qb0 + kv)),
                  edge_spec(kb0, -1), pair_spec(kb0), edge_spec(kb0, 2),
                  edge_spec(vb0, -1), pair_spec(vb0), edge_spec(vb0, 2)],
        out_specs=pl.BlockSpec((1, 2 * BLOCK, qw), lambda bi, kv, pair: (bi, pair, kv)),
        out_shape=jax.ShapeDtypeStruct((b, s, n_heads * HEAD_DIM), BF16),
        compiler_params=_params("parallel", "parallel", "parallel"),
        name="window_attn",
    )(slopes, sink, proj, proj, proj, proj, proj, proj, proj)


def _qkprep_body(x_ref, g_ref, cos_ref, sin_ref, o_ref, *, n_qchunks, heads_per_chunk, scale):
    j = pl.program_id(2)
    is_k = j >= n_qchunks
    gain = jnp.where(is_k, g_ref[1:2, :], g_ref[0:1, :])
    mult = jnp.where(is_k, 1.0, scale)
    cos = cos_ref[...]
    sin = sin_ref[...]
    lane = lax.broadcasted_iota(jnp.int32, cos.shape, 1)
    first = (lane % (HEAD_DIM // 2)) < (HEAD_DIM // 4)
    for hd in range(heads_per_chunk):
        x = x_ref[0, :, hd * HEAD_DIM:(hd + 1) * HEAD_DIM].astype(F32)
        ms = jnp.mean(x * x, axis=-1, keepdims=True)
        y = x * lax.rsqrt(ms + EPS) * gain
        partner = jnp.where(first, pltpu.roll(y, HEAD_DIM - HEAD_DIM // 4, 1),
                            pltpu.roll(y, HEAD_DIM // 4, 1))
        r = y * cos + partner * sin
        o_ref[0, :, hd * HEAD_DIM:(hd + 1) * HEAD_DIM] = (r * mult).astype(o_ref.dtype)


def _rope_tables(s):
    rows = s // GRID_W
    row = jnp.repeat(jnp.arange(rows), GRID_W).astype(F32)
    col = jnp.tile(jnp.arange(GRID_W), rows).astype(F32)
    quarter = HEAD_DIM // 4
    inv = ROPE_THETA ** (-jnp.arange(quarter, dtype=F32) / quarter)
    ang_r = row[:, None] * inv[None, :]
    ang_c = col[:, None] * inv[None, :]
    cos = jnp.concatenate([jnp.cos(ang_r)] * 2 + [jnp.cos(ang_c)] * 2, axis=-1)
    sin = jnp.concatenate([-jnp.sin(ang_r), jnp.sin(ang_r), -jnp.sin(ang_c), jnp.sin(ang_c)], axis=-1)
    return cos, sin


def _qk_prep(proj, gains, *, q_col, n_q_heads, n_k_heads, ts=256):
    b, s, _ = proj.shape
    cw = n_k_heads * HEAD_DIM
    assert q_col % cw == 0 and (n_q_heads * HEAD_DIM) % cw == 0
    n_qchunks = n_q_heads * HEAD_DIM // cw
    c0 = q_col // cw
    cos, sin = _rope_tables(s)
    return pl.pallas_call(
        functools.partial(_qkprep_body, n_qchunks=n_qchunks, heads_per_chunk=n_k_heads,
                          scale=HEAD_DIM ** -0.5 * LOG2E),
        grid=(b, s // ts, n_qchunks + 1),
        in_specs=[pl.BlockSpec((1, ts, cw), lambda bi, i, j: (bi, i, c0 + j)),
                  pl.BlockSpec((2, HEAD_DIM), lambda bi, i, j: (0, 0)),
                  pl.BlockSpec((ts, HEAD_DIM), lambda bi, i, j: (i, 0)),
                  pl.BlockSpec((ts, HEAD_DIM), lambda bi, i, j: (i, 0))],
        out_specs=pl.BlockSpec((1, ts, cw), lambda bi, i, j: (bi, i, j)),
        out_shape=jax.ShapeDtypeStruct((b, s, (n_qchunks + 1) * cw), BF16),
        compiler_params=_params("parallel", "parallel", "parallel"),
        name="qk_prep",
    )(proj, gains, cos, sin)


def _lane_groups(x):
    return [x[:, g * 128:(g + 1) * 128] for g in range(x.shape[1] // 128)]


def _grid_attn_body(q_ref, k_ref, v_ref, o_ref, *, tq, group, kc, s_len, unroll):
    q = q_ref[0]
    qs = jnp.concatenate([q[:, g * HEAD_DIM:(g + 1) * HEAD_DIM] for g in range(group)], axis=0)
    rows = group * tq

    def step(j, carry):
        m, lp, acc = carry
        off = pl.multiple_of(j * kc, kc)
        sg = _lane_groups(lax.dot_general(qs, k_ref[0, pl.ds(off, kc), :], _NT, preferred_element_type=F32))
        cmax = jnp.max(functools.reduce(jnp.maximum, sg), axis=-1, keepdims=True)
        m_new = jnp.maximum(m, jnp.broadcast_to(cmax, (rows, 128)))
        alpha = jnp.exp2(m - m_new)
        ps = [jnp.exp2(x - m_new) for x in sg]
        lp = alpha * lp + functools.reduce(jnp.add, ps)
        p = jnp.concatenate([pg.astype(BF16) for pg in ps], axis=1)
        acc = alpha * acc + jnp.dot(p, v_ref[0, pl.ds(off, kc), :], preferred_element_type=F32)
        return m_new, lp, acc

    assert HEAD_DIM == 128
    init = (jnp.full((rows, 128), NEG, F32), jnp.zeros((rows, 128), F32), jnp.zeros((rows, HEAD_DIM), F32))
    _, lp, acc = lax.fori_loop(0, s_len // kc, step, init, unroll=unroll)
    o = acc / jnp.sum(lp, axis=-1, keepdims=True)
    for g in range(group):
        o_ref[0, :, g * HEAD_DIM:(g + 1) * HEAD_DIM] = o[g * tq:(g + 1) * tq].astype(o_ref.dtype)


def _grid_attention(qk, proj, *, n_heads, n_kv, v_col, tq=256, kc=512, unroll=2):
    b, s, _ = qk.shape
    group = n_heads // n_kv
    qw = group * HEAD_DIM
    kc = min(kc, s)
    vb0 = v_col // HEAD_DIM
    return pl.pallas_call(
        functools.partial(_grid_attn_body, tq=tq, group=group, kc=kc, s_len=s, unroll=unroll),
        grid=(b, n_kv, s // tq),
        in_specs=[pl.BlockSpec((1, tq, qw), lambda bi, kv, i: (bi, i, kv)),
                  pl.BlockSpec((1, s, HEAD_DIM), lambda bi, kv, i: (bi, 0, n_heads + kv)),
                  pl.BlockSpec((1, s, HEAD_DIM), lambda bi, kv, i: (bi, 0, vb0 + kv))],
        out_specs=pl.BlockSpec((1, tq, qw), lambda bi, kv, i: (bi, i, kv)),
        out_shape=jax.ShapeDtypeStruct((b, s, n_heads * HEAD_DIM), BF16),
        compiler_params=_params("parallel", "parallel", "arbitrary"),
        name="grid_attn",
    )(qk, qk, proj)


def _diff_attn_body(slope_ref, lamc_ref, sub_ref, q_ref, k_ref, v_ref, o_ref, vt_ref,
                    *, tq, kc, s_len, lambda_init, unroll):
    h = pl.program_id(1)
    r = pl.program_id(2)
    slope2 = slope_ref[h] * LOG2E
    q = q_ref[0]
    zq = jnp.zeros((tq, DIFF_DIM), q.dtype)
    qd = jnp.concatenate([jnp.concatenate([q[:, :DIFF_DIM], zq], axis=1),
                          jnp.concatenate([zq, q[:, DIFF_DIM:]], axis=1)], axis=0)
    rows = 2 * tq
    n_chunks = s_len // kc
    i0 = r * tq
    c_diag = i0 // kc
    kj = lax.broadcasted_iota(jnp.int32, (kc, 128), 0).astype(F32)
    qi = (lax.broadcasted_iota(jnp.int32, (1, rows), 1) % tq).astype(F32)

    def off_diag(c):
        return c + (c >= c_diag).astype(jnp.int32)

    def terms(cc):
        is_left = cc < c_diag
        gap = jnp.where(is_left, i0 - (cc + 1) * kc, cc * kc - i0 - tq).astype(F32)
        keyterm = -slope2 * jnp.where(is_left, kc - kj, kj)
        qterm = -slope2 * (jnp.where(is_left, qi, tq - qi) + gap)
        return [keyterm] * (rows // 128), qterm

    def diag_bias():
        in_chunk = (i0 - c_diag * kc).astype(F32)
        kfull = lax.broadcasted_iota(jnp.int32, (kc, rows), 0).astype(F32)
        return _lane_groups(-slope2 * jnp.abs(qi + in_chunk - kfull))

    def probs(cc, bias_groups, qterm, m, l):
        off = pl.multiple_of(cc * kc, kc)
        st = lax.dot_general(k_ref[0, pl.ds(off, kc), :], qd, _NT, preferred_element_type=F32)
        ug = [x + bg for x, bg in zip(_lane_groups(st), bias_groups)]
        cmax = jnp.concatenate([jnp.max(x, axis=0, keepdims=True) for x in ug], axis=1)
        m_new = jnp.maximum(m, cmax + qterm)
        alpha = jnp.exp2(m - m_new)
        shift = m_new - qterm
        ps = [jnp.exp2(x - shift[:, g * 128:(g + 1) * 128]) for g, x in enumerate(ug)]
        l = alpha * l + jnp.concatenate([jnp.sum(x, axis=0, keepdims=True) for x in ps], axis=1)
        pt = jnp.concatenate([x.astype(BF16) for x in ps], axis=1)
        return m_new, l, alpha, pt

    @pl.when(r == 0)
    def _transpose_values():
        for c in range(n_chunks):
            vt_ref[c] = v_ref[0, c * kc:(c + 1) * kc, :].astype(F32).T.astype(BF16)

    def weighted_values(cc, alpha, pt, acc):
        return alpha * acc + jnp.dot(vt_ref[cc], pt, preferred_element_type=F32)

    def step(c, carry):
        m, l, acc, alpha, pt, prev = carry
        cc = off_diag(c)
        bias_groups, qterm = terms(cc)
        m, l, alpha_next, pt_next = probs(cc, bias_groups, qterm, m, l)
        acc = weighted_values(prev, alpha, pt, acc)
        return m, l, acc, alpha_next, pt_next, cc

    m0 = jnp.full((1, rows), NEG, F32)
    m, l, alpha, pt = probs(c_diag, diag_bias(), 0.0, m0, jnp.zeros((1, rows), F32))
    carry = (m, l, jnp.zeros((2 * DIFF_DIM, rows), F32), alpha, pt, c_diag)
    _, l, acc, alpha, pt, last = lax.fori_loop(0, n_chunks - 1, step, carry, unroll=unroll)
    ot = weighted_values(last, alpha, pt, acc) / l
    lf = lamc_ref[...]
    lam = (jnp.exp(jnp.sum(lf[0:1] * lf[1:2], axis=-1, keepdims=True))
           - jnp.exp(jnp.sum(lf[2:3] * lf[3:4], axis=-1, keepdims=True)) + lambda_init)
    dt = ot[:, :tq] - lam * ot[:, tq:]
    ms = jnp.mean(dt * dt, axis=0, keepdims=True)
    o = (dt * lax.rsqrt(ms + EPS)).T * sub_ref[...] * (1.0 - lambda_init)
    o_ref[0] = o.astype(o_ref.dtype)


def _diff_attention(proj, slopes, c_lambda, c_subnorm, lambda_init, *, n_heads, tq=256, kc=512, unroll=2):
    b, s, _ = proj.shape
    hw = 2 * DIFF_DIM
    kc = min(kc, s)
    tq = min(tq, kc)
    assert kc % tq == 0 and s % kc == 0
    return pl.pallas_call(
        functools.partial(_diff_attn_body, tq=tq, kc=kc, s_len=s, lambda_init=lambda_init, unroll=unroll),
        grid=(b, n_heads, s // tq),
        in_specs=[pl.BlockSpec(memory_space=pltpu.SMEM),
                  pl.BlockSpec((4, DIFF_DIM), lambda bi, h, i: (0, 0)),
                  pl.BlockSpec((1, hw), lambda bi, h, i: (0, 0)),
                  pl.BlockSpec((1, tq, hw), lambda bi, h, i: (bi, i, h)),
                  pl.BlockSpec((1, s, hw), lambda bi, h, i: (bi, 0, n_heads + h)),
                  pl.BlockSpec((1, s, hw), lambda bi, h, i: (bi, 0, 2 * n_heads + h))],
        out_specs=pl.BlockSpec((1, tq, hw), lambda bi, h, i: (bi, i, h)),
        out_shape=jax.ShapeDtypeStruct((b, s, n_heads * hw), BF16),
        scratch_shapes=[pltpu.VMEM((s // kc, hw, kc), BF16)],
        compiler_params=_params("arbitrary", "arbitrary", "arbitrary"),
        name="diff_attn",
    )(slopes, c_lambda, c_subnorm.reshape(1, hw), proj, proj, proj)


def _router_body(h_ref, g_ref, wrt_ref, aff_ref):
    x = h_ref[...]
    ms = jnp.mean(x * x, axis=-1, keepdims=True)
    xn = x * lax.rsqrt(ms + EPS) * g_ref[...]
    logits = lax.dot_general(wrt_ref[...], xn, _NT, precision=lax.Precision.HIGHEST,
                             preferred_element_type=F32)
    m = jnp.max(logits, axis=0, keepdims=True)
    e = jnp.exp(logits - m)
    aff_ref[0] = e / jnp.sum(e, axis=0, keepdims=True)


def _router(h2d, g, w_router, b, s, tm=512):
    t, d = h2d.shape
    tm = min(tm, s)
    spb = s // tm
    ne = w_router.shape[1]
    return pl.pallas_call(
        _router_body,
        grid=(t // tm,),
        in_specs=[pl.BlockSpec((tm, d), lambda i: (i, 0)),
                  pl.BlockSpec((1, d), lambda i: (0, 0)),
                  pl.BlockSpec((ne, d), lambda i: (0, 0))],
        out_specs=pl.BlockSpec((1, ne, tm), lambda i: (i // spb, 0, i % spb)),
        out_shape=jax.ShapeDtypeStruct((b, ne, s), F32),
        compiler_params=_params("parallel"),
        name="router",
    )(h2d, g.reshape(1, d), w_router.T)


def _cumsum_lanes(mask, tri):
    rows, n = mask.shape
    carry = jnp.zeros((rows, 1), F32)
    out = []
    for blk in range(n // 128):
        c = jnp.dot(mask[:, blk * 128:(blk + 1) * 128].astype(BF16), tri, preferred_element_type=F32) + carry
        out.append(c)
        carry = c[:, 127:128]
    return jnp.concatenate(out, axis=1)


def _select_body(aff_ref, pos_ref, *, cap):
    a = aff_ref[0]
    bits = pltpu.bitcast(a, jnp.int32)
    ne = a.shape[0]
    thr = jnp.zeros((ne, 1), jnp.int32)
    for bit in range(30, -1, -1):
        cand = thr | (1 << bit)
        cnt = jnp.sum((bits >= cand).astype(jnp.int32), axis=1, keepdims=True)
        thr = jnp.where(cnt >= cap, cand, thr)
    gt = bits > thr
    eq = bits == thr
    need = (cap - jnp.sum(gt.astype(jnp.int32), axis=1, keepdims=True)).astype(F32)
    r = lax.broadcasted_iota(jnp.int32, (128, 128), 0)
    c = lax.broadcasted_iota(jnp.int32, (128, 128), 1)
    tri = (r <= c).astype(BF16)
    sel = gt | (eq & (_cumsum_lanes(eq.astype(F32), tri) <= need))
    pos = _cumsum_lanes(sel.astype(F32), tri) - 1.0
    pos_ref[0] = jnp.where(sel, pos, -1.0)


def _select(aff, cap):
    b, ne, s = aff.shape
    return pl.pallas_call(
        functools.partial(_select_body, cap=cap),
        grid=(b,),
        in_specs=[pl.BlockSpec((1, ne, s), lambda bi: (bi, 0, 0))],
        out_specs=pl.BlockSpec((1, ne, s), lambda bi: (bi, 0, 0)),
        out_shape=jax.ShapeDtypeStruct((b, ne, s), F32),
        compiler_params=_params("parallel"),
        name="select",
    )(aff)


def _compact_body(pos_ref, aff_ref, idx_ref, gate_ref, *, tc):
    c0 = pl.program_id(1) * tc
    pos = pos_ref[0]
    s = pos.shape[1]
    slot = (c0 + lax.broadcasted_iota(jnp.int32, (tc, s), 0)).astype(F32)
    tok = lax.broadcasted_iota(jnp.int32, (tc, s), 1)
    hit = pos == slot
    idx_ref[0] = jnp.sum(jnp.where(hit, tok, 0), axis=1, keepdims=True)
    gate_ref[0] = jnp.sum(jnp.where(hit, aff_ref[0], 0.0), axis=1, keepdims=True)


def _compact(pos, aff, cap, tc=128):
    b, ne, s = pos.shape
    tc = min(tc, cap)
    row = pl.BlockSpec((1, 1, s), lambda r, j: (r, 0, 0))
    col = pl.BlockSpec((1, tc, 1), lambda r, j: (r, j, 0))
    return pl.pallas_call(
        functools.partial(_compact_body, tc=tc),
        grid=(b * ne, cap // tc),
        in_specs=[row, row],
        out_specs=[col, col],
        out_shape=[jax.ShapeDtypeStruct((b * ne, cap, 1), jnp.int32),
                   jax.ShapeDtypeStruct((b * ne, cap, 1), F32)],
        compiler_params=_params("parallel", "parallel"),
        name="compact",
    )(pos.reshape(b * ne, 1, s), aff.reshape(b * ne, 1, s))


def _ffn_body(idx_ref, h_hbm, g_ref, gate_ref, wg_ref, wu_ref, wd_ref, o_ref, land, xg, hid, sems,
              *, cap, s_len, n_b, n_f, n_d):
    e = pl.program_id(0)
    step = pl.program_id(1)
    ne = pl.num_programs(0)
    rows = n_b * cap
    piece = rows // n_d
    cur = e % 2

    def row_copy(src_row, slot, r):
        return pltpu.make_async_copy(h_hbm.at[pl.ds(src_row, 1)], land.at[slot, pl.ds(r, 1)], sems.at[slot])

    def for_piece(fn, unrolled):
        if unrolled:
            for r in range(piece):
                fn(r)
        else:
            def body(r, carry):
                fn(r)
                return carry
            lax.fori_loop(0, piece, body, 0, unroll=8)

    def issue(ex, p, slot, unrolled):
        bi, c0 = (p * piece) // cap, (p * piece) % cap
        base = (bi * ne + ex) * cap + c0
        for_piece(lambda r: row_copy(bi * s_len + idx_ref[base + r], slot, r).start(), unrolled)

    def drain(slot, unrolled):
        for_piece(lambda r: row_copy(0, slot, r).wait(), unrolled)

    def norm_into(slot, half, p):
        x = land[slot]
        ms = jnp.mean(x * x, axis=-1, keepdims=True)
        xg[half, pl.ds(pl.multiple_of(p * piece, piece), piece), :] = (
            x * lax.rsqrt(ms + EPS) * g_ref[...]).astype(BF16)

    @pl.when((e == 0) & (step == 0))
    def _prologue():
        issue(0, 0, 0, False)
        for p in range(n_d):
            if p + 1 < n_d:
                issue(0, p + 1, (p + 1) % 2, False)
            drain(p % 2, False)
            norm_into(p % 2, 0, p)
        issue(jnp.minimum(1, ne - 1), 0, 0, False)

    @pl.when(step < n_f)
    def _up():
        x = xg[cur]
        gt = jnp.dot(x, wg_ref[0, 0].astype(BF16), preferred_element_type=F32)
        up = jnp.dot(x, wu_ref[0, 0].astype(BF16), preferred_element_type=F32)
        hid[step] = (gt * jax.nn.sigmoid(gt) * up).astype(BF16)

    @pl.when(step >= n_f)
    def _down():
        hcat = jnp.concatenate([hid[f] for f in range(n_f)], axis=1)
        y = jnp.dot(hcat, wd_ref[0, 0].astype(BF16), preferred_element_type=F32)
        o_ref[0] = y * gate_ref[:, 0].reshape(rows, 1)
        j = step - n_f
        slot = j % 2
        drain(slot, True)
        norm_into(slot, 1 - cur, j)
        following = (j + 1) % n_d
        issue(jnp.minimum(e + 1 + (j + 1) // n_d, ne - 1), following, 1 - slot, True)

    @pl.when((e == ne - 1) & (step == n_f + n_d - 1))
    def _drain_tail():
        drain(0, False)


def _expert_ffn(idx_flat, h2d, g, gates, w_gate, w_up, w_down, layer, b, s, cap, tf=256, tn=512):
    t, d = h2d.shape
    _, ne, _, dff = w_gate.shape
    tf = min(tf, dff)
    tn = min(tn, d)
    n_f = dff // tf
    n_d = d // tn
    rows = b * cap
    piece = rows // n_d
    assert n_d % 2 == 0 and cap % piece == 0 and piece % 16 == 0
    grid_spec = pltpu.PrefetchScalarGridSpec(
        num_scalar_prefetch=1,
        grid=(ne, n_f + n_d),
        in_specs=[pl.BlockSpec(memory_space=pl.ANY),
                  pl.BlockSpec((1, d), lambda e, st, idx: (0, 0)),
                  pl.BlockSpec((b, 1, cap, 1), lambda e, st, idx: (0, e, 0, 0)),
                  pl.BlockSpec((1, 1, d, tf), lambda e, st, idx: (layer, e, 0, jnp.minimum(st, n_f - 1))),
                  pl.BlockSpec((1, 1, d, tf), lambda e, st, idx: (layer, e, 0, jnp.minimum(st, n_f - 1))),
                  pl.BlockSpec((1, 1, dff, tn), lambda e, st, idx: (layer, e, 0, jnp.maximum(st - n_f, 0)))],
        out_specs=pl.BlockSpec((1, rows, tn), lambda e, st, idx: (e, 0, jnp.maximum(st - n_f, 0))),
        scratch_shapes=[pltpu.VMEM((2, piece, d), F32), pltpu.VMEM((2, rows, d), BF16),
                        pltpu.VMEM((n_f, rows, tf), BF16), pltpu.SemaphoreType.DMA((2,))],
    )
    return pl.pallas_call(
        functools.partial(_ffn_body, cap=cap, s_len=s, n_b=b, n_f=n_f, n_d=n_d),
        grid_spec=grid_spec,
        out_shape=jax.ShapeDtypeStruct((ne, rows, d), F32),
        compiler_params=_params("arbitrary", "arbitrary"),
        name="expert_ffn",
    )(idx_flat, h2d, g.reshape(1, d), gates.reshape(b, ne, cap, 1), w_gate, w_up, w_down)


def _combine_body(idx_ref, h_hbm, y_ref, o_hbm, buf, gsem, ssem, *, cap, n_b):
    del h_hbm
    e = pl.program_id(0)
    b = pl.program_id(1)
    ne = pl.num_programs(0)
    n_steps = ne * n_b
    step = e * n_b + b
    slot = step % 2

    def gather_row(c, bi, tok, sl):
        return pltpu.make_async_copy(o_hbm.at[bi, pl.ds(tok, 1)], buf.at[sl, pl.ds(c, 1)], gsem.at[sl])

    def scatter_row(c, tok, sl):
        return pltpu.make_async_copy(buf.at[sl, pl.ds(c, 1)], o_hbm.at[b, pl.ds(tok, 1)], ssem.at[sl])

    def for_rows(fn):
        def body(c, carry):
            fn(c)
            return carry
        lax.fori_loop(0, cap, body, 0, unroll=8)

    def start_gather(st, sl):
        e2, b2 = st // n_b, st % n_b
        base2 = (b2 * ne + e2) * cap
        for_rows(lambda c: gather_row(c, b2, idx_ref[base2 + c], sl).start())

    def wait_scatter(sl):
        for_rows(lambda c: scatter_row(c, 0, sl).wait())

    pl.when(step == 0)(lambda: start_gather(0, 0))
    for_rows(lambda c: gather_row(c, b, 0, slot).wait())
    buf[slot] = buf[slot] + y_ref[0]
    base = (b * ne + e) * cap
    for_rows(lambda c: scatter_row(c, idx_ref[base + c], slot).start())
    if n_b == 1:
        wait_scatter(slot)
    else:
        pl.when(step > 0)(lambda: wait_scatter(1 - slot))
    pl.when(step + 1 < n_steps)(lambda: start_gather(step + 1, 1 - slot))
    if n_b > 1:
        pl.when(step == n_steps - 1)(lambda: wait_scatter(slot))


def _combine(idx_flat, h, y, cap):
    b, s, d = h.shape
    ne = y.shape[0]
    grid_spec = pltpu.PrefetchScalarGridSpec(
        num_scalar_prefetch=1,
        grid=(ne, b),
        in_specs=[pl.BlockSpec(memory_space=pl.ANY),
                  pl.BlockSpec((1, cap, d), lambda e, bi, idx: (e, bi, 0))],
        out_specs=pl.BlockSpec(memory_space=pl.ANY),
        scratch_shapes=[pltpu.VMEM((2, cap, d), F32), pltpu.SemaphoreType.DMA((2,)),
                        pltpu.SemaphoreType.DMA((2,))],
    )
    return pl.pallas_call(
        functools.partial(_combine_body, cap=cap, n_b=b),
        grid_spec=grid_spec,
        out_shape=jax.ShapeDtypeStruct((b, s, d), F32),
        input_output_aliases={1: 0},
        compiler_params=_params("arbitrary", "arbitrary"),
        name="combine",
    )(idx_flat, h, y)


def _moe(h, g, w_router, w_gate, w_up, w_down, layer):
    b, s, d = h.shape
    ne = w_router.shape[1]
    cap = CAPACITY_FACTOR * s // ne
    h2d = h.reshape(b * s, d)
    aff = _router(h2d, g, w_router, b, s)
    pos = _select(aff, cap)
    idx, gates = _compact(pos, aff, cap)
    idx_flat = idx.reshape(b * ne * cap)
    y = _expert_ffn(idx_flat, h2d, g, gates, w_gate, w_up, w_down, layer, b, s, cap)
    return _combine(idx_flat, h, y, cap)


def _alibi_slopes(n):
    return 2.0 ** (-8.0 * jnp.arange(1, n + 1, dtype=F32) / n)


def _mm_tiles(m, n):
    tm = next(t for t in (1024, 512, 256, 128, 8) if m % t == 0)
    tn = next(t for t in (512, 256, 128) if n % t == 0)
    return tm, tn


def _even_mixer(h, g, w_in, w_out, sink, qnorm, knorm):
    b, s, d = h.shape
    t = b * s
    n_heads = d // HEAD_DIM
    ha, hb = n_heads // 2, n_heads // 2
    kva, kvb = ha // 4, hb // 4
    q_a, kv_a, q_b, kv_b = ha * HEAD_DIM, kva * HEAD_DIM, hb * HEAD_DIM, kvb * HEAD_DIM
    h2d = h.reshape(t, d)
    hn = _rmsnorm(h2d, g, BF16)
    w_total = w_in.shape[1]
    tm, tn = _mm_tiles(t, w_total)
    col_scale = jnp.where(jnp.arange(w_total) < q_a, HEAD_DIM ** -0.5 * LOG2E, 1.0).astype(F32)
    proj = _matmul([hn], w_in, None, BF16, tm, tn, "in_proj_even", col_scale).reshape(b, s, w_total)
    out_a = _window_attention(proj, _alibi_slopes(ha), sink, n_heads=ha, n_kv=kva,
                              q_col=0, k_col=q_a, v_col=q_a + kv_a)
    qb0 = q_a + 2 * kv_a
    qk = _qk_prep(proj, jnp.stack([qnorm, knorm]), q_col=qb0, n_q_heads=hb, n_k_heads=kvb)
    out_b = _grid_attention(qk, proj, n_heads=hb, n_kv=kvb, v_col=qb0 + q_b + kv_b)
    tm, tn = _mm_tiles(t, d)
    out = _matmul([out_a.reshape(t, q_a), out_b.reshape(t, q_b)], w_out, h2d, F32, tm, tn, "out_proj_even")
    return out.reshape(b, s, d)


def _odd_mixer(h, g, w_in, w_out, c_lambda, c_subnorm, lambda_init):
    b, s, d = h.shape
    t = b * s
    n_heads = d // (2 * DIFF_DIM)
    h2d = h.reshape(t, d)
    hn = _rmsnorm(h2d, g, BF16)
    w_total = w_in.shape[1]
    tm, tn = _mm_tiles(t, w_total)
    col_scale = jnp.where(jnp.arange(w_total) < d, DIFF_DIM ** -0.5 * LOG2E, 1.0).astype(F32)
    proj = _matmul([hn], w_in, None, BF16, tm, tn, "in_proj_odd", col_scale).reshape(b, s, w_total)
    mix = _diff_attention(proj, _alibi_slopes(n_heads), c_lambda, c_subnorm, lambda_init, n_heads=n_heads)
    tm, tn = _mm_tiles(t, d)
    out = _matmul([mix.reshape(t, d)], w_out, h2d, F32, tm, tn, "out_proj_odd")
    return out.reshape(b, s, d)


def kernel(x, norm_mix, norm_ffn, norm_final, w_in_even, w_out_even, sink_a, qnorm_b, knorm_b,
           w_in_odd, w_out_odd, c_lambda, c_subnorm, w_router, w_gate, w_up, w_down):
    b, s, d = x.shape
    depth = norm_mix.shape[0]
    h = x
    for layer in range(depth):
        i = layer // 2
        if layer % 2 == 0:
            h = _even_mixer(h, norm_mix[layer], w_in_even[i], w_out_even[i], sink_a[i], qnorm_b[i], knorm_b[i])
        else:
            lambda_init = 0.8 - 0.6 * math.exp(-0.3 * layer)
            h = _odd_mixer(h, norm_mix[layer], w_in_odd[i], w_out_odd[i], c_lambda[i], c_subnorm[i], lambda_init)
        h = _moe(h, norm_ffn[layer], w_router[layer], w_gate, w_up, w_down, layer)
    return _rmsnorm(h.reshape(b * s, d), norm_final, F32).reshape(b, s, d)
```

```python
import functools
import math

import jax
import jax.numpy as jnp
from jax import lax
from jax.experimental import pallas as pl
from jax.experimental.pallas import tpu as pltpu

F32 = jnp.float32
BF16 = jnp.bfloat16

HEAD_DIM = 128
WINDOW = 128
BLOCK = 128
GRID_W = 64
ROPE_THETA = 10000.0
DIFF_DIM = 128
N_EXPERTS = 16
CAPACITY_FACTOR = 2
EPS = 1e-6
NEG = -1e30
LOG2E = math.log2(math.e)

VMEM_LIMIT_BYTES = 56 * 1024 * 1024

_NT = (((1,), (1,)), ((), ()))


def _params(*sem):
    return pltpu.CompilerParams(dimension_semantics=sem, vmem_limit_bytes=VMEM_LIMIT_BYTES)


def _rmsnorm_body(x_ref, g_ref, o_ref):
    x = x_ref[...]
    ms = jnp.mean(x * x, axis=-1, keepdims=True)
    o_ref[...] = (x * lax.rsqrt(ms + EPS) * g_ref[...]).astype(o_ref.dtype)


def _rmsnorm(x2d, g, out_dtype, tm=256):
    t, d = x2d.shape
    return pl.pallas_call(
        _rmsnorm_body,
        grid=(t // tm,),
        in_specs=[pl.BlockSpec((tm, d), lambda i: (i, 0)),
                  pl.BlockSpec((1, d), lambda i: (0, 0))],
        out_specs=pl.BlockSpec((tm, d), lambda i: (i, 0)),
        out_shape=jax.ShapeDtypeStruct((t, d), out_dtype),
        compiler_params=_params("parallel"),
        name="rmsnorm",
    )(x2d, g.reshape(1, d))


def _rope_heads(acc, gain, mult, cos, sin):
    lane = lax.broadcasted_iota(jnp.int32, cos.shape, 1)
    first = (lane % (HEAD_DIM // 2)) < (HEAD_DIM // 4)
    out = []
    for x in _lane_groups(acc):
        ms = jnp.mean(x * x, axis=-1, keepdims=True)
        y = x * lax.rsqrt(ms + EPS) * gain
        partner = jnp.where(first, pltpu.roll(y, HEAD_DIM - HEAD_DIM // 4, 1), pltpu.roll(y, HEAD_DIM // 4, 1))
        out.append((y * cos + partner * sin) * mult)
    return jnp.concatenate(out, axis=1)


def _mm_body(*refs, n_pairs, has_res, has_scale, rope):
    n_in = 2 * n_pairs + int(has_res) + int(has_scale) + (3 if rope else 0)
    o_ref = refs[n_in]
    wbf = refs[n_in + 1:]

    @pl.when(pl.program_id(1) == 0)
    def _cast():
        for p in range(n_pairs):
            wbf[p][...] = refs[2 * p + 1][...].astype(BF16)

    acc = None
    for p in range(n_pairs):
        d = jnp.dot(refs[2 * p][...], wbf[p][...], preferred_element_type=F32)
        acc = d if acc is None else acc + d
    if has_res:
        acc = acc + refs[2 * n_pairs][...]
    if has_scale:
        acc = acc * refs[2 * n_pairs + int(has_res)][...]
    if rope is None:
        o_ref[...] = acc.astype(o_ref.dtype)
    else:
        lo, n_q, q_mult = rope
        g_ref, cos_ref, sin_ref = refs[n_in - 3:n_in]
        j = pl.program_id(0)
        is_k = j == lo + n_q
        roped = (j >= lo) & (j <= lo + n_q)

        @pl.when(roped)
        def _():
            gain = jnp.where(is_k, g_ref[1:2, :], g_ref[0:1, :])
            mult = jnp.where(is_k, 1.0, q_mult)
            o_ref[...] = _rope_heads(acc, gain, mult, cos_ref[...], sin_ref[...]).astype(o_ref.dtype)

        @pl.when(jnp.logical_not(roped))
        def _():
            o_ref[...] = acc.astype(o_ref.dtype)


def _matmul(a_list, w, res, out_dtype, tm, tn, name, col_scale=None, rope=None):
    m = a_list[0].shape[0]
    n = w.shape[1]
    in_specs, args = [], []
    kp = a_list[0].shape[1]
    assert all(a.shape == (m, kp) for a in a_list) and w.shape[0] == kp * len(a_list)
    for part, a in enumerate(a_list):
        in_specs.append(pl.BlockSpec((tm, kp), lambda j, i: (i, 0)))
        in_specs.append(pl.BlockSpec((kp, tn), functools.partial(lambda j, i, r: (r, j), r=part)))
        args += [a, w]
    if res is not None:
        in_specs.append(pl.BlockSpec((tm, tn), lambda j, i: (i, j)))
        args.append(res)
    if col_scale is not None:
        in_specs.append(pl.BlockSpec((1, tn), lambda j, i: (0, j)))
        args.append(col_scale.reshape(1, n))
    if rope is not None:
        lo, n_q, q_mult, gains, cos, sin = rope
        seq_blocks = cos.shape[0] // tm
        assert cos.shape[0] % tm == 0 and tn % HEAD_DIM == 0
        in_specs.append(pl.BlockSpec(gains.shape, lambda j, i: (0, 0)))
        in_specs.append(pl.BlockSpec((tm, HEAD_DIM), lambda j, i: (i % seq_blocks, 0)))
        in_specs.append(pl.BlockSpec((tm, HEAD_DIM), lambda j, i: (i % seq_blocks, 0)))
        args += [gains, cos, sin]
        rope = (lo, n_q, q_mult)
    return pl.pallas_call(
        functools.partial(_mm_body, n_pairs=len(a_list), has_res=res is not None,
                          has_scale=col_scale is not None, rope=rope),
        grid=(n // tn, m // tm),
        in_specs=in_specs,
        out_specs=pl.BlockSpec((tm, tn), lambda j, i: (i, j)),
        out_shape=jax.ShapeDtypeStruct((m, n), out_dtype),
        scratch_shapes=[pltpu.VMEM((kp, tn), BF16) for _ in a_list],
        compiler_params=_params("arbitrary", "arbitrary"),
        name=name,
    )(*args)


def _win_body(slope_ref, sink_ref, q_ref, kp_ref, kc_ref, kn_ref, vp_ref, vc_ref, vn_ref, o_ref,
              *, nb, group):
    kv = pl.program_id(1)
    pair = pl.program_id(2)
    tq, band = 2 * BLOCK, 4 * BLOCK
    kb = jnp.concatenate([kp_ref[0], kc_ref[0], kn_ref[0]], axis=0)
    vb = jnp.concatenate([vp_ref[0], vc_ref[0], vn_ref[0]], axis=0)
    kj = lax.broadcasted_iota(jnp.int32, (band, tq), 0)
    qi = lax.broadcasted_iota(jnp.int32, (band, tq), 1)
    arel = jnp.abs(BLOCK + qi - kj)
    kpos = (2 * pair - 1) * BLOCK + kj
    valid = (arel <= WINDOW) & (kpos >= 0) & (kpos < nb * BLOCK)
    arel_f = arel.astype(F32)
    heads = [kv * group + g for g in range(group)]
    q = q_ref[0]
    for g, h in enumerate(heads):
        bias = jnp.where(valid, -(slope_ref[h] * LOG2E) * arel_f, NEG)
        sk = sink_ref[h] * LOG2E
        s = lax.dot_general(kb, q[:, g * HEAD_DIM:(g + 1) * HEAD_DIM], _NT,
                            preferred_element_type=F32) + bias
        m = jnp.maximum(jnp.max(s, axis=0, keepdims=True), sk)
        p = jnp.exp2(s - m)
        den = jnp.sum(p, axis=0, keepdims=True) + jnp.exp2(sk - m)
        ot = lax.dot_general(vb, p.astype(BF16), (((0,), (0,)), ((), ())),
                             preferred_element_type=F32) / den
        o_ref[0, :, g * HEAD_DIM:(g + 1) * HEAD_DIM] = ot.T.astype(o_ref.dtype)


def _window_attention(proj, slopes, sink, *, n_heads, n_kv, q_col, k_col, v_col):
    b, s, _ = proj.shape
    nb = s // BLOCK
    group = n_heads // n_kv
    qw = group * HEAD_DIM
    qb0, kb0, vb0 = q_col // qw, k_col // HEAD_DIM, v_col // HEAD_DIM
    assert q_col % qw == 0 and k_col % HEAD_DIM == 0 and v_col % HEAD_DIM == 0

    assert nb % 2 == 0

    def edge_spec(col0, shift):
        def imap(bi, kv, pair):
            return (bi, jnp.clip(2 * pair + shift, 0, nb - 1), col0 + kv)
        return pl.BlockSpec((1, BLOCK, HEAD_DIM), imap)

    def pair_spec(col0):
        return pl.BlockSpec((1, 2 * BLOCK, HEAD_DIM), lambda bi, kv, pair: (bi, pair, col0 + kv))

    smem = pl.BlockSpec(memory_space=pltpu.SMEM)
    return pl.pallas_call(
        functools.partial(_win_body, nb=nb, group=group),
        grid=(b, n_kv, nb // 2),
        in_specs=[smem, smem,
                  pl.BlockSpec((1, 2 * BLOCK, qw), lambda bi, kv, pair: (bi, pair, qb0 + kv)),
                  edge_spec(kb0, -1), pair_spec(kb0), edge_spec(kb0, 2),
                  edge_spec(vb0, -1), pair_spec(vb0), edge_spec(vb0, 2)],
        out_specs=pl.BlockSpec((1, 2 * BLOCK, qw), lambda bi, kv, pair: (bi, pair, kv)),
        out_shape=jax.ShapeDtypeStruct((b, s, n_heads * HEAD_DIM), BF16),
        compiler_params=_params("parallel", "parallel", "parallel"),
        name="window_attn",
    )(slopes, sink, proj, proj, proj, proj, proj, proj, proj)


def _qkprep_body(x_ref, g_ref, cos_ref, sin_ref, o_ref, *, n_qchunks, heads_per_chunk, scale):
    j = pl.program_id(2)
    is_k = j >= n_qchunks
    gain = jnp.where(is_k, g_ref[1:2, :], g_ref[0:1, :])
    mult = jnp.where(is_k, 1.0, scale)
    cos = cos_ref[...]
    sin = sin_ref[...]
    lane = lax.broadcasted_iota(jnp.int32, cos.shape, 1)
    first = (lane % (HEAD_DIM // 2)) < (HEAD_DIM // 4)
    for hd in range(heads_per_chunk):
        x = x_ref[0, :, hd * HEAD_DIM:(hd + 1) * HEAD_DIM].astype(F32)
        ms = jnp.mean(x * x, axis=-1, keepdims=True)
        y = x * lax.rsqrt(ms + EPS) * gain
        partner = jnp.where(first, pltpu.roll(y, HEAD_DIM - HEAD_DIM // 4, 1),
                            pltpu.roll(y, HEAD_DIM // 4, 1))
        r = y * cos + partner * sin
        o_ref[0, :, hd * HEAD_DIM:(hd + 1) * HEAD_DIM] = (r * mult).astype(o_ref.dtype)


def _rope_tables(s):
    rows = s // GRID_W
    row = jnp.repeat(jnp.arange(rows), GRID_W).astype(F32)
    col = jnp.tile(jnp.arange(GRID_W), rows).astype(F32)
    quarter = HEAD_DIM // 4
    inv = ROPE_THETA ** (-jnp.arange(quarter, dtype=F32) / quarter)
    ang_r = row[:, None] * inv[None, :]
    ang_c = col[:, None] * inv[None, :]
    cos = jnp.concatenate([jnp.cos(ang_r)] * 2 + [jnp.cos(ang_c)] * 2, axis=-1)
    sin = jnp.concatenate([-jnp.sin(ang_r), jnp.sin(ang_r), -jnp.sin(ang_c), jnp.sin(ang_c)], axis=-1)
    return cos, sin


def _qk_prep(proj, gains, *, q_col, n_q_heads, n_k_heads, ts=256):
    b, s, _ = proj.shape
    cw = n_k_heads * HEAD_DIM
    assert q_col % cw == 0 and (n_q_heads * HEAD_DIM) % cw == 0
    n_qchunks = n_q_heads * HEAD_DIM // cw
    c0 = q_col // cw
    cos, sin = _rope_tables(s)
    return pl.pallas_call(
        functools.partial(_qkprep_body, n_qchunks=n_qchunks, heads_per_chunk=n_k_heads,
                          scale=HEAD_DIM ** -0.5 * LOG2E),
        grid=(b, s // ts, n_qchunks + 1),
        in_specs=[pl.BlockSpec((1, ts, cw), lambda bi, i, j: (bi, i, c0 + j)),
                  pl.BlockSpec((2, HEAD_DIM), lambda bi, i, j: (0, 0)),
                  pl.BlockSpec((ts, HEAD_DIM), lambda bi, i, j: (i, 0)),
                  pl.BlockSpec((ts, HEAD_DIM), lambda bi, i, j: (i, 0))],
        out_specs=pl.BlockSpec((1, ts, cw), lambda bi, i, j: (bi, i, j)),
        out_shape=jax.ShapeDtypeStruct((b, s, (n_qchunks + 1) * cw), BF16),
        compiler_params=_params("parallel", "parallel", "parallel"),
        name="qk_prep",
    )(proj, gains, cos, sin)


def _lane_groups(x):
    return [x[:, g * 128:(g + 1) * 128] for g in range(x.shape[1] // 128)]


def _grid_attn_body(q_ref, k_ref, v_ref, o_ref, *, tq, group, kc, s_len, unroll):
    q = q_ref[0]
    qs = jnp.concatenate([q[:, g * HEAD_DIM:(g + 1) * HEAD_DIM] for g in range(group)], axis=0)
    rows = group * tq

    def step(j, carry):
        m, lp, acc = carry
        off = pl.multiple_of(j * kc, kc)
        sg = _lane_groups(lax.dot_general(qs, k_ref[0, pl.ds(off, kc), :], _NT, preferred_element_type=F32))
        cmax = jnp.max(functools.reduce(jnp.maximum, sg), axis=-1, keepdims=True)
        m_new = jnp.maximum(m, jnp.broadcast_to(cmax, (rows, 128)))
        alpha = jnp.exp2(m - m_new)
        ps = [jnp.exp2(x - m_new) for x in sg]
        lp = alpha * lp + functools.reduce(jnp.add, ps)
        p = jnp.concatenate([pg.astype(BF16) for pg in ps], axis=1)
        acc = alpha * acc + jnp.dot(p, v_ref[0, pl.ds(off, kc), :], preferred_element_type=F32)
        return m_new, lp, acc

    assert HEAD_DIM == 128
    init = (jnp.full((rows, 128), NEG, F32), jnp.zeros((rows, 128), F32), jnp.zeros((rows, HEAD_DIM), F32))
    _, lp, acc = lax.fori_loop(0, s_len // kc, step, init, unroll=unroll)
    o = acc / jnp.sum(lp, axis=-1, keepdims=True)
    for g in range(group):
        o_ref[0, :, g * HEAD_DIM:(g + 1) * HEAD_DIM] = o[g * tq:(g + 1) * tq].astype(o_ref.dtype)


def _grid_attention(qk, q_col, k_col, proj, *, n_heads, n_kv, v_col, tq=256, kc=512, unroll=2):
    b, s, _ = qk.shape
    group = n_heads // n_kv
    qw = group * HEAD_DIM
    kc = min(kc, s)
    assert q_col % qw == 0 and k_col % HEAD_DIM == 0 and v_col % HEAD_DIM == 0
    qb0, kb0, vb0 = q_col // qw, k_col // HEAD_DIM, v_col // HEAD_DIM
    return pl.pallas_call(
        functools.partial(_grid_attn_body, tq=tq, group=group, kc=kc, s_len=s, unroll=unroll),
        grid=(b, n_kv, s // tq),
        in_specs=[pl.BlockSpec((1, tq, qw), lambda bi, kv, i: (bi, i, qb0 + kv)),
                  pl.BlockSpec((1, s, HEAD_DIM), lambda bi, kv, i: (bi, 0, kb0 + kv)),
                  pl.BlockSpec((1, s, HEAD_DIM), lambda bi, kv, i: (bi, 0, vb0 + kv))],
        out_specs=pl.BlockSpec((1, tq, qw), lambda bi, kv, i: (bi, i, kv)),
        out_shape=jax.ShapeDtypeStruct((b, s, n_heads * HEAD_DIM), BF16),
        compiler_params=_params("parallel", "parallel", "arbitrary"),
        name="grid_attn",
    )(qk, qk, proj)


def _diff_attn_body(slope_ref, lamc_ref, sub_ref, q_ref, k_ref, v_ref, o_ref, vt_ref,
                    *, tq, kc, s_len, lambda_init, unroll):
    h = pl.program_id(1)
    r = pl.program_id(2)
    slope2 = slope_ref[h] * LOG2E
    q = q_ref[0]
    zq = jnp.zeros((tq, DIFF_DIM), q.dtype)
    qd = jnp.concatenate([jnp.concatenate([q[:, :DIFF_DIM], zq], axis=1),
                          jnp.concatenate([zq, q[:, DIFF_DIM:]], axis=1)], axis=0)
    rows = 2 * tq
    n_chunks = s_len // kc
    i0 = r * tq
    c_diag = i0 // kc
    kj = lax.broadcasted_iota(jnp.int32, (kc, 128), 0).astype(F32)
    qi = (lax.broadcasted_iota(jnp.int32, (1, rows), 1) % tq).astype(F32)

    def off_diag(c):
        return c + (c >= c_diag).astype(jnp.int32)

    def terms(cc):
        is_left = cc < c_diag
        gap = jnp.where(is_left, i0 - (cc + 1) * kc, cc * kc - i0 - tq).astype(F32)
        keyterm = -slope2 * jnp.where(is_left, kc - kj, kj)
        qterm = -slope2 * (jnp.where(is_left, qi, tq - qi) + gap)
        return [keyterm] * (rows // 128), qterm

    def diag_bias():
        in_chunk = (i0 - c_diag * kc).astype(F32)
        kfull = lax.broadcasted_iota(jnp.int32, (kc, rows), 0).astype(F32)
        return _lane_groups(-slope2 * jnp.abs(qi + in_chunk - kfull))

    def probs(cc, bias_groups, qterm, m, l):
        off = pl.multiple_of(cc * kc, kc)
        st = lax.dot_general(k_ref[0, pl.ds(off, kc), :], qd, _NT, preferred_element_type=F32)
        ug = [x + bg for x, bg in zip(_lane_groups(st), bias_groups)]
        cmax = jnp.concatenate([jnp.max(x, axis=0, keepdims=True) for x in ug], axis=1)
        m_new = jnp.maximum(m, cmax + qterm)
        alpha = jnp.exp2(m - m_new)
        shift = m_new - qterm
        ps = [jnp.exp2(x - shift[:, g * 128:(g + 1) * 128]) for g, x in enumerate(ug)]
        l = alpha * l + jnp.concatenate([jnp.sum(x, axis=0, keepdims=True) for x in ps], axis=1)
        pt = jnp.concatenate([x.astype(BF16) for x in ps], axis=1)
        return m_new, l, alpha, pt

    @pl.when(r == 0)
    def _transpose_values():
        for c in range(n_chunks):
            vt_ref[c] = v_ref[0, c * kc:(c + 1) * kc, :].astype(F32).T.astype(BF16)

    def weighted_values(cc, alpha, pt, acc):
        return alpha * acc + jnp.dot(vt_ref[cc], pt, preferred_element_type=F32)

    def step(c, carry):
        m, l, acc, alpha, pt, prev = carry
        cc = off_diag(c)
        bias_groups, qterm = terms(cc)
        m, l, alpha_next, pt_next = probs(cc, bias_groups, qterm, m, l)
        acc = weighted_values(prev, alpha, pt, acc)
        return m, l, acc, alpha_next, pt_next, cc

    m0 = jnp.full((1, rows), NEG, F32)
    m, l, alpha, pt = probs(c_diag, diag_bias(), 0.0, m0, jnp.zeros((1, rows), F32))
    carry = (m, l, jnp.zeros((2 * DIFF_DIM, rows), F32), alpha, pt, c_diag)
    _, l, acc, alpha, pt, last = lax.fori_loop(0, n_chunks - 1, step, carry, unroll=unroll)
    ot = weighted_values(last, alpha, pt, acc) / l
    lf = lamc_ref[...]
    lam = (jnp.exp(jnp.sum(lf[0:1] * lf[1:2], axis=-1, keepdims=True))
           - jnp.exp(jnp.sum(lf[2:3] * lf[3:4], axis=-1, keepdims=True)) + lambda_init)
    dt = ot[:, :tq] - lam * ot[:, tq:]
    ms = jnp.mean(dt * dt, axis=0, keepdims=True)
    o = (dt * lax.rsqrt(ms + EPS)).T * sub_ref[...] * (1.0 - lambda_init)
    o_ref[0] = o.astype(o_ref.dtype)


def _diff_attention(proj, slopes, c_lambda, c_subnorm, lambda_init, *, n_heads, tq=256, kc=512, unroll=2):
    b, s, _ = proj.shape
    hw = 2 * DIFF_DIM
    kc = min(kc, s)
    tq = min(tq, kc)
    assert kc % tq == 0 and s % kc == 0
    return pl.pallas_call(
        functools.partial(_diff_attn_body, tq=tq, kc=kc, s_len=s, lambda_init=lambda_init, unroll=unroll),
        grid=(b, n_heads, s // tq),
        in_specs=[pl.BlockSpec(memory_space=pltpu.SMEM),
                  pl.BlockSpec((4, DIFF_DIM), lambda bi, h, i: (0, 0)),
                  pl.BlockSpec((1, hw), lambda bi, h, i: (0, 0)),
                  pl.BlockSpec((1, tq, hw), lambda bi, h, i: (bi, i, h)),
                  pl.BlockSpec((1, s, hw), lambda bi, h, i: (bi, 0, n_heads + h)),
                  pl.BlockSpec((1, s, hw), lambda bi, h, i: (bi, 0, 2 * n_heads + h))],
        out_specs=pl.BlockSpec((1, tq, hw), lambda bi, h, i: (bi, i, h)),
        out_shape=jax.ShapeDtypeStruct((b, s, n_heads * hw), BF16),
        scratch_shapes=[pltpu.VMEM((s // kc, hw, kc), BF16)],
        compiler_params=_params("arbitrary", "arbitrary", "arbitrary"),
        name="diff_attn",
    )(slopes, c_lambda, c_subnorm.reshape(1, hw), proj, proj, proj)


def _router_body(h_ref, g_ref, wrt_ref, aff_ref):
    x = h_ref[...]
    ms = jnp.mean(x * x, axis=-1, keepdims=True)
    xn = x * lax.rsqrt(ms + EPS) * g_ref[...]
    logits = lax.dot_general(wrt_ref[...], xn, _NT, precision=lax.Precision.HIGHEST,
                             preferred_element_type=F32)
    m = jnp.max(logits, axis=0, keepdims=True)
    e = jnp.exp(logits - m)
    aff_ref[0] = e / jnp.sum(e, axis=0, keepdims=True)


def _router(h2d, g, w_router, b, s, tm=512):
    t, d = h2d.shape
    tm = min(tm, s)
    spb = s // tm
    ne = w_router.shape[1]
    return pl.pallas_call(
        _router_body,
        grid=(t // tm,),
        in_specs=[pl.BlockSpec((tm, d), lambda i: (i, 0)),
                  pl.BlockSpec((1, d), lambda i: (0, 0)),
                  pl.BlockSpec((ne, d), lambda i: (0, 0))],
        out_specs=pl.BlockSpec((1, ne, tm), lambda i: (i // spb, 0, i % spb)),
        out_shape=jax.ShapeDtypeStruct((b, ne, s), F32),
        compiler_params=_params("parallel"),
        name="router",
    )(h2d, g.reshape(1, d), w_router.T)


def _cumsum_lanes(mask, tri):
    rows, n = mask.shape
    carry = jnp.zeros((rows, 1), F32)
    out = []
    for blk in range(n // 128):
        c = jnp.dot(mask[:, blk * 128:(blk + 1) * 128].astype(BF16), tri, preferred_element_type=F32) + carry
        out.append(c)
        carry = c[:, 127:128]
    return jnp.concatenate(out, axis=1)


def _select_body(aff_ref, pos_ref, *, cap):
    a = aff_ref[0]
    bits = pltpu.bitcast(a, jnp.int32)
    ne = a.shape[0]
    thr = jnp.zeros((ne, 1), jnp.int32)
    for bit in range(30, -1, -1):
        cand = thr | (1 << bit)
        cnt = jnp.sum((bits >= cand).astype(jnp.int32), axis=1, keepdims=True)
        thr = jnp.where(cnt >= cap, cand, thr)
    gt = bits > thr
    eq = bits == thr
    need = (cap - jnp.sum(gt.astype(jnp.int32), axis=1, keepdims=True)).astype(F32)
    r = lax.broadcasted_iota(jnp.int32, (128, 128), 0)
    c = lax.broadcasted_iota(jnp.int32, (128, 128), 1)
    tri = (r <= c).astype(BF16)
    sel = gt | (eq & (_cumsum_lanes(eq.astype(F32), tri) <= need))
    pos = _cumsum_lanes(sel.astype(F32), tri) - 1.0
    pos_ref[0] = jnp.where(sel, pos, -1.0)


def _select(aff, cap):
    b, ne, s = aff.shape
    return pl.pallas_call(
        functools.partial(_select_body, cap=cap),
        grid=(b,),
        in_specs=[pl.BlockSpec((1, ne, s), lambda bi: (bi, 0, 0))],
        out_specs=pl.BlockSpec((1, ne, s), lambda bi: (bi, 0, 0)),
        out_shape=jax.ShapeDtypeStruct((b, ne, s), F32),
        compiler_params=_params("parallel"),
        name="select",
    )(aff)


def _compact_body(pos_ref, aff_ref, idx_ref, gate_ref, *, tc):
    c0 = pl.program_id(1) * tc
    pos = pos_ref[0]
    s = pos.shape[1]
    slot = (c0 + lax.broadcasted_iota(jnp.int32, (tc, s), 0)).astype(F32)
    tok = lax.broadcasted_iota(jnp.int32, (tc, s), 1)
    hit = pos == slot
    idx_ref[0] = jnp.sum(jnp.where(hit, tok, 0), axis=1, keepdims=True)
    gate_ref[0] = jnp.sum(jnp.where(hit, aff_ref[0], 0.0), axis=1, keepdims=True)


def _compact(pos, aff, cap, tc=128):
    b, ne, s = pos.shape
    tc = min(tc, cap)
    row = pl.BlockSpec((1, 1, s), lambda r, j: (r, 0, 0))
    col = pl.BlockSpec((1, tc, 1), lambda r, j: (r, j, 0))
    return pl.pallas_call(
        functools.partial(_compact_body, tc=tc),
        grid=(b * ne, cap // tc),
        in_specs=[row, row],
        out_specs=[col, col],
        out_shape=[jax.ShapeDtypeStruct((b * ne, cap, 1), jnp.int32),
                   jax.ShapeDtypeStruct((b * ne, cap, 1), F32)],
        compiler_params=_params("parallel", "parallel"),
        name="compact",
    )(pos.reshape(b * ne, 1, s), aff.reshape(b * ne, 1, s))


def _ffn_body(idx_ref, h_hbm, g_ref, gate_ref, wg_ref, wu_ref, wd_ref, o_ref, land, xg, hid, sems,
              *, cap, s_len, n_b, n_f, n_d):
    e = pl.program_id(0)
    step = pl.program_id(1)
    ne = pl.num_programs(0)
    rows = n_b * cap
    piece = rows // n_d
    cur = e % 2

    def row_copy(src_row, slot, r):
        return pltpu.make_async_copy(h_hbm.at[pl.ds(src_row, 1)], land.at[slot, pl.ds(r, 1)], sems.at[slot])

    def for_piece(fn, unrolled):
        if unrolled:
            for r in range(piece):
                fn(r)
        else:
            def body(r, carry):
                fn(r)
                return carry
            lax.fori_loop(0, piece, body, 0, unroll=8)

    def issue(ex, p, slot, unrolled):
        bi, c0 = (p * piece) // cap, (p * piece) % cap
        base = (bi * ne + ex) * cap + c0
        for_piece(lambda r: row_copy(bi * s_len + idx_ref[base + r], slot, r).start(), unrolled)

    def drain(slot, unrolled):
        del unrolled
        pltpu.make_async_copy(h_hbm.at[pl.ds(0, piece)], land.at[slot], sems.at[slot]).wait()

    def norm_into(slot, half, p):
        x = land[slot]
        ms = jnp.mean(x * x, axis=-1, keepdims=True)
        xg[half, pl.ds(pl.multiple_of(p * piece, piece), piece), :] = (
            x * lax.rsqrt(ms + EPS) * g_ref[...]).astype(BF16)

    @pl.when((e == 0) & (step == 0))
    def _prologue():
        issue(0, 0, 0, False)
        for p in range(n_d):
            if p + 1 < n_d:
                issue(0, p + 1, (p + 1) % 2, False)
            drain(p % 2, False)
            norm_into(p % 2, 0, p)
        issue(jnp.minimum(1, ne - 1), 0, 0, False)

    @pl.when(step < n_f)
    def _up():
        x = xg[cur]
        gt = jnp.dot(x, wg_ref[0, 0].astype(BF16), preferred_element_type=F32)
        up = jnp.dot(x, wu_ref[0, 0].astype(BF16), preferred_element_type=F32)
        hid[step] = (gt * jax.nn.sigmoid(gt) * up).astype(BF16)

    @pl.when(step >= n_f)
    def _down():
        hcat = jnp.concatenate([hid[f] for f in range(n_f)], axis=1)
        y = jnp.dot(hcat, wd_ref[0, 0].astype(BF16), preferred_element_type=F32)
        o_ref[0] = y * gate_ref[:, 0].reshape(rows, 1)
        j = step - n_f
        slot = j % 2
        drain(slot, True)
        norm_into(slot, 1 - cur, j)
        following = (j + 1) % n_d
        issue(jnp.minimum(e + 1 + (j + 1) // n_d, ne - 1), following, 1 - slot, True)

    @pl.when((e == ne - 1) & (step == n_f + n_d - 1))
    def _drain_tail():
        drain(0, False)


def _expert_ffn(idx_flat, h2d, g, gates, w_gate, w_up, w_down, layer, b, s, cap, tf=256, tn=512):
    t, d = h2d.shape
    _, ne, _, dff = w_gate.shape
    tf = min(tf, dff)
    tn = min(tn, d)
    n_f = dff // tf
    n_d = d // tn
    rows = b * cap
    piece = rows // n_d
    assert n_d % 2 == 0 and cap % piece == 0 and piece % 16 == 0
    grid_spec = pltpu.PrefetchScalarGridSpec(
        num_scalar_prefetch=1,
        grid=(ne, n_f + n_d),
        in_specs=[pl.BlockSpec(memory_space=pl.ANY),
                  pl.BlockSpec((1, d), lambda e, st, idx: (0, 0)),
                  pl.BlockSpec((b, 1, cap, 1), lambda e, st, idx: (0, e, 0, 0)),
                  pl.BlockSpec((1, 1, d, tf), lambda e, st, idx: (layer, e, 0, jnp.minimum(st, n_f - 1))),
                  pl.BlockSpec((1, 1, d, tf), lambda e, st, idx: (layer, e, 0, jnp.minimum(st, n_f - 1))),
                  pl.BlockSpec((1, 1, dff, tn), lambda e, st, idx: (layer, e, 0, jnp.maximum(st - n_f, 0)))],
        out_specs=pl.BlockSpec((1, rows, tn), lambda e, st, idx: (e, 0, jnp.maximum(st - n_f, 0))),
        scratch_shapes=[pltpu.VMEM((2, piece, d), F32), pltpu.VMEM((2, rows, d), BF16),
                        pltpu.VMEM((n_f, rows, tf), BF16), pltpu.SemaphoreType.DMA((2,))],
    )
    return pl.pallas_call(
        functools.partial(_ffn_body, cap=cap, s_len=s, n_b=b, n_f=n_f, n_d=n_d),
        grid_spec=grid_spec,
        out_shape=jax.ShapeDtypeStruct((ne, rows, d), F32),
        compiler_params=_params("arbitrary", "arbitrary"),
        name="expert_ffn",
    )(idx_flat, h2d, g.reshape(1, d), gates.reshape(b, ne, cap, 1), w_gate, w_up, w_down)


def _combine_body(idx_ref, h_hbm, y_ref, o_hbm, buf, gsem, ssem, *, cap, n_b):
    del h_hbm
    e = pl.program_id(0)
    b = pl.program_id(1)
    ne = pl.num_programs(0)
    n_steps = ne * n_b
    step = e * n_b + b
    slot = step % 2

    def gather_row(c, bi, tok, sl):
        return pltpu.make_async_copy(o_hbm.at[bi, pl.ds(tok, 1)], buf.at[sl, pl.ds(c, 1)], gsem.at[sl])

    def scatter_row(c, tok, sl):
        return pltpu.make_async_copy(buf.at[sl, pl.ds(c, 1)], o_hbm.at[b, pl.ds(tok, 1)], ssem.at[sl])

    def for_rows(fn):
        def body(c, carry):
            fn(c)
            return carry
        lax.fori_loop(0, cap, body, 0, unroll=8)

    def start_gather(st, sl):
        e2, b2 = st // n_b, st % n_b
        base2 = (b2 * ne + e2) * cap
        for_rows(lambda c: gather_row(c, b2, idx_ref[base2 + c], sl).start())

    def wait_scatter(sl):
        pltpu.make_async_copy(buf.at[sl], o_hbm.at[b, pl.ds(0, cap)], ssem.at[sl]).wait()

    pl.when(step == 0)(lambda: start_gather(0, 0))
    pltpu.make_async_copy(o_hbm.at[b, pl.ds(0, cap)], buf.at[slot], gsem.at[slot]).wait()
    buf[slot] = buf[slot] + y_ref[0]
    base = (b * ne + e) * cap
    for_rows(lambda c: scatter_row(c, idx_ref[base + c], slot).start())
    if n_b == 1:
        wait_scatter(slot)
    else:
        pl.when(step > 0)(lambda: wait_scatter(1 - slot))
    pl.when(step + 1 < n_steps)(lambda: start_gather(step + 1, 1 - slot))
    if n_b > 1:
        pl.when(step == n_steps - 1)(lambda: wait_scatter(slot))


def _combine(idx_flat, h, y, cap):
    b, s, d = h.shape
    ne = y.shape[0]
    grid_spec = pltpu.PrefetchScalarGridSpec(
        num_scalar_prefetch=1,
        grid=(ne, b),
        in_specs=[pl.BlockSpec(memory_space=pl.ANY),
                  pl.BlockSpec((1, cap, d), lambda e, bi, idx: (e, bi, 0))],
        out_specs=pl.BlockSpec(memory_space=pl.ANY),
        scratch_shapes=[pltpu.VMEM((2, cap, d), F32), pltpu.SemaphoreType.DMA((2,)),
                        pltpu.SemaphoreType.DMA((2,))],
    )
    return pl.pallas_call(
        functools.partial(_combine_body, cap=cap, n_b=b),
        grid_spec=grid_spec,
        out_shape=jax.ShapeDtypeStruct((b, s, d), F32),
        input_output_aliases={1: 0},
        compiler_params=_params("arbitrary", "arbitrary"),
        name="combine",
    )(idx_flat, h, y)


def _moe(h, g, w_router, w_gate, w_up, w_down, layer):
    b, s, d = h.shape
    ne = w_router.shape[1]
    cap = CAPACITY_FACTOR * s // ne
    h2d = h.reshape(b * s, d)
    aff = _router(h2d, g, w_router, b, s)
    pos = _select(aff, cap)
    idx, gates = _compact(pos, aff, cap)
    idx_flat = idx.reshape(b * ne * cap)
    y = _expert_ffn(idx_flat, h2d, g, gates, w_gate, w_up, w_down, layer, b, s, cap)
    return _combine(idx_flat, h, y, cap)


def _alibi_slopes(n):
    return 2.0 ** (-8.0 * jnp.arange(1, n + 1, dtype=F32) / n)


def _mm_tiles(m, n):
    tm = next(t for t in (1024, 512, 256, 128, 8) if m % t == 0)
    tn = next(t for t in (512, 256, 128) if n % t == 0)
    return tm, tn


def _even_mixer(h, g, w_in, w_out, sink, qnorm, knorm):
    b, s, d = h.shape
    t = b * s
    n_heads = d // HEAD_DIM
    ha, hb = n_heads // 2, n_heads // 2
    kva, kvb = ha // 4, hb // 4
    q_a, kv_a, q_b, kv_b = ha * HEAD_DIM, kva * HEAD_DIM, hb * HEAD_DIM, kvb * HEAD_DIM
    h2d = h.reshape(t, d)
    hn = _rmsnorm(h2d, g, BF16)
    w_total = w_in.shape[1]
    tm, tn = _mm_tiles(t, w_total)
    col_scale = jnp.where(jnp.arange(w_total) < q_a, HEAD_DIM ** -0.5 * LOG2E, 1.0).astype(F32)
    qb0 = q_a + 2 * kv_a
    gains = jnp.stack([qnorm, knorm])
    q_mult = HEAD_DIM ** -0.5 * LOG2E
    fused = tn == kv_b and qb0 % tn == 0 and s % tm == 0
    rope = (qb0 // tn, q_b // tn, q_mult, gains) + _rope_tables(s) if fused else None
    proj = _matmul([hn], w_in, None, BF16, tm, tn, "in_proj_even", col_scale, rope).reshape(b, s, w_total)
    out_a = _window_attention(proj, _alibi_slopes(ha), sink, n_heads=ha, n_kv=kva,
                              q_col=0, k_col=q_a, v_col=q_a + kv_a)
    if fused:
        qk, q_col, k_col = proj, qb0, qb0 + q_b
    else:
        qk, q_col, k_col = _qk_prep(proj, gains, q_col=qb0, n_q_heads=hb, n_k_heads=kvb), 0, q_b
    out_b = _grid_attention(qk, q_col, k_col, proj, n_heads=hb, n_kv=kvb, v_col=qb0 + q_b + kv_b)
    tm, tn = _mm_tiles(t, d)
    out = _matmul([out_a.reshape(t, q_a), out_b.reshape(t, q_b)], w_out, h2d, F32, tm, tn, "out_proj_even")
    return out.reshape(b, s, d)


def _odd_mixer(h, g, w_in, w_out, c_lambda, c_subnorm, lambda_init):
    b, s, d = h.shape
    t = b * s
    n_heads = d // (2 * DIFF_DIM)
    h2d = h.reshape(t, d)
    hn = _rmsnorm(h2d, g, BF16)
    w_total = w_in.shape[1]
    tm, tn = _mm_tiles(t, w_total)
    col_scale = jnp.where(jnp.arange(w_total) < d, DIFF_DIM ** -0.5 * LOG2E, 1.0).astype(F32)
    proj = _matmul([hn], w_in, None, BF16, tm, tn, "in_proj_odd", col_scale).reshape(b, s, w_total)
    mix = _diff_attention(proj, _alibi_slopes(n_heads), c_lambda, c_subnorm, lambda_init, n_heads=n_heads)
    tm, tn = _mm_tiles(t, d)
    out = _matmul([mix.reshape(t, d)], w_out, h2d, F32, tm, tn, "out_proj_odd")
    return out.reshape(b, s, d)


def kernel(x, norm_mix, norm_ffn, norm_final, w_in_even, w_out_even, sink_a, qnorm_b, knorm_b,
           w_in_odd, w_out_odd, c_lambda, c_subnorm, w_router, w_gate, w_up, w_down):
    b, s, d = x.shape
    depth = norm_mix.shape[0]
    h = x
    for layer in range(depth):
        i = layer // 2
        if layer % 2 == 0:
            h = _even_mixer(h, norm_mix[layer], w_in_even[i], w_out_even[i], sink_a[i], qnorm_b[i], knorm_b[i])
        else:
            lambda_init = 0.8 - 0.6 * math.exp(-0.3 * layer)
            h = _odd_mixer(h, norm_mix[layer], w_in_odd[i], w_out_odd[i], c_lambda[i], c_subnorm[i], lambda_init)
        h = _moe(h, norm_ffn[layer], w_router[layer], w_gate, w_up, w_down, layer)
    return _rmsnorm(h.reshape(b * s, d), norm_final, F32).reshape(b, s, d)
```

```python
import functools
import math

import jax
import jax.numpy as jnp
from jax import lax
from jax.experimental import pallas as pl
from jax.experimental.pallas import tpu as pltpu

F32 = jnp.float32
BF16 = jnp.bfloat16

HEAD_DIM = 128
WINDOW = 128
BLOCK = 128
GRID_W = 64
ROPE_THETA = 10000.0
DIFF_DIM = 128
N_EXPERTS = 16
CAPACITY_FACTOR = 2
EPS = 1e-6
NEG = -1e30
LOG2E = math.log2(math.e)

VMEM_LIMIT_BYTES = 56 * 1024 * 1024

_NT = (((1,), (1,)), ((), ()))


def _params(*sem):
    return pltpu.CompilerParams(dimension_semantics=sem, vmem_limit_bytes=VMEM_LIMIT_BYTES)


def _rmsnorm_body(x_ref, g_ref, o_ref):
    x = x_ref[...]
    ms = jnp.mean(x * x, axis=-1, keepdims=True)
    o_ref[...] = (x * lax.rsqrt(ms + EPS) * g_ref[...]).astype(o_ref.dtype)


def _rmsnorm(x2d, g, out_dtype, tm=256):
    t, d = x2d.shape
    return pl.pallas_call(
        _rmsnorm_body,
        grid=(t // tm,),
        in_specs=[pl.BlockSpec((tm, d), lambda i: (i, 0)),
                  pl.BlockSpec((1, d), lambda i: (0, 0))],
        out_specs=pl.BlockSpec((tm, d), lambda i: (i, 0)),
        out_shape=jax.ShapeDtypeStruct((t, d), out_dtype),
        compiler_params=_params("parallel"),
        name="rmsnorm",
    )(x2d, g.reshape(1, d))


def _rope_heads(acc, gain, mult, cos, sin):
    lane = lax.broadcasted_iota(jnp.int32, cos.shape, 1)
    first = (lane % (HEAD_DIM // 2)) < (HEAD_DIM // 4)
    out = []
    for x in _lane_groups(acc):
        ms = jnp.mean(x * x, axis=-1, keepdims=True)
        y = x * lax.rsqrt(ms + EPS) * gain
        partner = jnp.where(first, pltpu.roll(y, HEAD_DIM - HEAD_DIM // 4, 1), pltpu.roll(y, HEAD_DIM // 4, 1))
        out.append((y * cos + partner * sin) * mult)
    return jnp.concatenate(out, axis=1)


def _mm_body(*refs, n_pairs, has_res, has_scale, rope):
    n_in = 2 * n_pairs + int(has_res) + int(has_scale) + (3 if rope else 0)
    o_ref = refs[n_in]
    wbf = refs[n_in + 1:]

    @pl.when(pl.program_id(1) == 0)
    def _cast():
        for p in range(n_pairs):
            wbf[p][...] = refs[2 * p + 1][...].astype(BF16)

    acc = None
    for p in range(n_pairs):
        d = jnp.dot(refs[2 * p][...], wbf[p][...], preferred_element_type=F32)
        acc = d if acc is None else acc + d
    if has_res:
        acc = acc + refs[2 * n_pairs][...]
    if has_scale:
        acc = acc * refs[2 * n_pairs + int(has_res)][...]
    if rope is None:
        o_ref[...] = acc.astype(o_ref.dtype)
    else:
        lo, n_q, q_mult = rope
        g_ref, cos_ref, sin_ref = refs[n_in - 3:n_in]
        j = pl.program_id(0)
        is_k = j == lo + n_q
        roped = (j >= lo) & (j <= lo + n_q)

        @pl.when(roped)
        def _():
            gain = jnp.where(is_k, g_ref[1:2, :], g_ref[0:1, :])
            mult = jnp.where(is_k, 1.0, q_mult)
            o_ref[...] = _rope_heads(acc, gain, mult, cos_ref[...], sin_ref[...]).astype(o_ref.dtype)

        @pl.when(jnp.logical_not(roped))
        def _():
            o_ref[...] = acc.astype(o_ref.dtype)


def _matmul(a_list, w, res, out_dtype, tm, tn, name, col_scale=None, rope=None):
    m = a_list[0].shape[0]
    n = w.shape[1]
    in_specs, args = [], []
    kp = a_list[0].shape[1]
    assert all(a.shape == (m, kp) for a in a_list) and w.shape[0] == kp * len(a_list)
    for part, a in enumerate(a_list):
        in_specs.append(pl.BlockSpec((tm, kp), lambda j, i: (i, 0)))
        in_specs.append(pl.BlockSpec((kp, tn), functools.partial(lambda j, i, r: (r, j), r=part)))
        args += [a, w]
    if res is not None:
        in_specs.append(pl.BlockSpec((tm, tn), lambda j, i: (i, j)))
        args.append(res)
    if col_scale is not None:
        in_specs.append(pl.BlockSpec((1, tn), lambda j, i: (0, j)))
        args.append(col_scale.reshape(1, n))
    if rope is not None:
        lo, n_q, q_mult, gains, cos, sin = rope
        seq_blocks = cos.shape[0] // tm
        assert cos.shape[0] % tm == 0 and tn % HEAD_DIM == 0
        in_specs.append(pl.BlockSpec(gains.shape, lambda j, i: (0, 0)))
        in_specs.append(pl.BlockSpec((tm, HEAD_DIM), lambda j, i: (i % seq_blocks, 0)))
        in_specs.append(pl.BlockSpec((tm, HEAD_DIM), lambda j, i: (i % seq_blocks, 0)))
        args += [gains, cos, sin]
        rope = (lo, n_q, q_mult)
    return pl.pallas_call(
        functools.partial(_mm_body, n_pairs=len(a_list), has_res=res is not None,
                          has_scale=col_scale is not None, rope=rope),
        grid=(n // tn, m // tm),
        in_specs=in_specs,
        out_specs=pl.BlockSpec((tm, tn), lambda j, i: (i, j)),
        out_shape=jax.ShapeDtypeStruct((m, n), out_dtype),
        scratch_shapes=[pltpu.VMEM((kp, tn), BF16) for _ in a_list],
        compiler_params=_params("arbitrary", "arbitrary"),
        name=name,
    )(*args)


def _win_body(slope_ref, sink_ref, q_ref, kp_ref, kc_ref, kn_ref, vp_ref, vc_ref, vn_ref, o_ref,
              *, nb, group):
    kv = pl.program_id(1)
    pair = pl.program_id(2)
    tq, band = 2 * BLOCK, 4 * BLOCK
    kb = jnp.concatenate([kp_ref[0], kc_ref[0], kn_ref[0]], axis=0)
    vb = jnp.concatenate([vp_ref[0], vc_ref[0], vn_ref[0]], axis=0)
    kj = lax.broadcasted_iota(jnp.int32, (band, tq), 0)
    qi = lax.broadcasted_iota(jnp.int32, (band, tq), 1)
    arel = jnp.abs(BLOCK + qi - kj)
    kpos = (2 * pair - 1) * BLOCK + kj
    valid = (arel <= WINDOW) & (kpos >= 0) & (kpos < nb * BLOCK)
    arel_f = arel.astype(F32)
    heads = [kv * group + g for g in range(group)]
    q = q_ref[0]
    for g, h in enumerate(heads):
        bias = jnp.where(valid, -(slope_ref[h] * LOG2E) * arel_f, NEG)
        sk = sink_ref[h] * LOG2E
        s = lax.dot_general(kb, q[:, g * HEAD_DIM:(g + 1) * HEAD_DIM], _NT,
                            preferred_element_type=F32) + bias
        m = jnp.maximum(jnp.max(s, axis=0, keepdims=True), sk)
        p = jnp.exp2(s - m)
        den = jnp.sum(p, axis=0, keepdims=True) + jnp.exp2(sk - m)
        ot = lax.dot_general(vb, p.astype(BF16), (((0,), (0,)), ((), ())),
                             preferred_element_type=F32) / den
        o_ref[0, :, g * HEAD_DIM:(g + 1) * HEAD_DIM] = ot.T.astype(o_ref.dtype)


def _window_attention(proj, slopes, sink, *, n_heads, n_kv, q_col, k_col, v_col):
    b, s, _ = proj.shape
    nb = s // BLOCK
    group = n_heads // n_kv
    qw = group * HEAD_DIM
    qb0, kb0, vb0 = q_col // qw, k_col // HEAD_DIM, v_col // HEAD_DIM
    assert q_col % qw == 0 and k_col % HEAD_DIM == 0 and v_col % HEAD_DIM == 0

    assert nb % 2 == 0

    def edge_spec(col0, shift):
        def imap(bi, kv, pair):
            return (bi, jnp.clip(2 * pair + shift, 0, nb - 1), col0 + kv)
        return pl.BlockSpec((1, BLOCK, HEAD_DIM), imap)

    def pair_spec(col0):
        return pl.BlockSpec((1, 2 * BLOCK, HEAD_DIM), lambda bi, kv, pair: (bi, pair, col0 + kv))

    smem = pl.BlockSpec(memory_space=pltpu.SMEM)
    return pl.pallas_call(
        functools.partial(_win_body, nb=nb, group=group),
        grid=(b, n_kv, nb // 2),
        in_specs=[smem, smem,
                  pl.BlockSpec((1, 2 * BLOCK, qw), lambda bi, kv, pair: (bi, pair, qb0 + kv)),
                  edge_spec(kb0, -1), pair_spec(kb0), edge_spec(kb0, 2),
                  edge_spec(vb0, -1), pair_spec(vb0), edge_spec(vb0, 2)],
        out_specs=pl.BlockSpec((1, 2 * BLOCK, qw), lambda bi, kv, pair: (bi, pair, kv)),
        out_shape=jax.ShapeDtypeStruct((b, s, n_heads * HEAD_DIM), BF16),
        compiler_params=_params("parallel", "parallel", "parallel"),
        name="window_attn",
    )(slopes, sink, proj, proj, proj, proj, proj, proj, proj)


def _qkprep_body(x_ref, g_ref, cos_ref, sin_ref, o_ref, *, n_qchunks, heads_per_chunk, scale):
    j = pl.program_id(2)
    is_k = j >= n_qchunks
    gain = jnp.where(is_k, g_ref[1:2, :], g_ref[0:1, :])
    mult = jnp.where(is_k, 1.0, scale)
    cos = cos_ref[...]
    sin = sin_ref[...]
    lane = lax.broadcasted_iota(jnp.int32, cos.shape, 1)
    first = (lane % (HEAD_DIM // 2)) < (HEAD_DIM // 4)
    for hd in range(heads_per_chunk):
        x = x_ref[0, :, hd * HEAD_DIM:(hd + 1) * HEAD_DIM].astype(F32)
        ms = jnp.mean(x * x, axis=-1, keepdims=True)
        y = x * lax.rsqrt(ms + EPS) * gain
        partner = jnp.where(first, pltpu.roll(y, HEAD_DIM - HEAD_DIM // 4, 1),
                            pltpu.roll(y, HEAD_DIM // 4, 1))
        r = y * cos + partner * sin
        o_ref[0, :, hd * HEAD_DIM:(hd + 1) * HEAD_DIM] = (r * mult).astype(o_ref.dtype)


def _rope_tables(s):
    rows = s // GRID_W
    row = jnp.repeat(jnp.arange(rows), GRID_W).astype(F32)
    col = jnp.tile(jnp.arange(GRID_W), rows).astype(F32)
    quarter = HEAD_DIM // 4
    inv = ROPE_THETA ** (-jnp.arange(quarter, dtype=F32) / quarter)
    ang_r = row[:, None] * inv[None, :]
    ang_c = col[:, None] * inv[None, :]
    cos = jnp.concatenate([jnp.cos(ang_r)] * 2 + [jnp.cos(ang_c)] * 2, axis=-1)
    sin = jnp.concatenate([-jnp.sin(ang_r), jnp.sin(ang_r), -jnp.sin(ang_c), jnp.sin(ang_c)], axis=-1)
    return cos, sin


def _qk_prep(proj, gains, *, q_col, n_q_heads, n_k_heads, ts=256):
    b, s, _ = proj.shape
    cw = n_k_heads * HEAD_DIM
    assert q_col % cw == 0 and (n_q_heads * HEAD_DIM) % cw == 0
    n_qchunks = n_q_heads * HEAD_DIM // cw
    c0 = q_col // cw
    cos, sin = _rope_tables(s)
    return pl.pallas_call(
        functools.partial(_qkprep_body, n_qchunks=n_qchunks, heads_per_chunk=n_k_heads,
                          scale=HEAD_DIM ** -0.5 * LOG2E),
        grid=(b, s // ts, n_qchunks + 1),
        in_specs=[pl.BlockSpec((1, ts, cw), lambda bi, i, j: (bi, i, c0 + j)),
                  pl.BlockSpec((2, HEAD_DIM), lambda bi, i, j: (0, 0)),
                  pl.BlockSpec((ts, HEAD_DIM), lambda bi, i, j: (i, 0)),
                  pl.BlockSpec((ts, HEAD_DIM), lambda bi, i, j: (i, 0))],
        out_specs=pl.BlockSpec((1, ts, cw), lambda bi, i, j: (bi, i, j)),
        out_shape=jax.ShapeDtypeStruct((b, s, (n_qchunks + 1) * cw), BF16),
        compiler_params=_params("parallel", "parallel", "parallel"),
        name="qk_prep",
    )(proj, gains, cos, sin)


def _lane_groups(x):
    return [x[:, g * 128:(g + 1) * 128] for g in range(x.shape[1] // 128)]


def _grid_attn_body(q_ref, k_ref, v_ref, o_ref, *, tq, group, kc, s_len, unroll):
    q = q_ref[0]
    qs = jnp.concatenate([q[:, g * HEAD_DIM:(g + 1) * HEAD_DIM] for g in range(group)], axis=0)
    rows = group * tq

    def step(j, carry):
        m, lp, acc = carry
        off = pl.multiple_of(j * kc, kc)
        sg = _lane_groups(lax.dot_general(qs, k_ref[0, pl.ds(off, kc), :], _NT, preferred_element_type=F32))
        cmax = jnp.max(functools.reduce(jnp.maximum, sg), axis=-1, keepdims=True)
        m_new = jnp.maximum(m, jnp.broadcast_to(cmax, (rows, 128)))
        alpha = jnp.exp2(m - m_new)
        ps = [jnp.exp2(x - m_new) for x in sg]
        lp = alpha * lp + functools.reduce(jnp.add, ps)
        p = jnp.concatenate([pg.astype(BF16) for pg in ps], axis=1)
        acc = alpha * acc + jnp.dot(p, v_ref[0, pl.ds(off, kc), :], preferred_element_type=F32)
        return m_new, lp, acc

    assert HEAD_DIM == 128
    init = (jnp.full((rows, 128), NEG, F32), jnp.zeros((rows, 128), F32), jnp.zeros((rows, HEAD_DIM), F32))
    _, lp, acc = lax.fori_loop(0, s_len // kc, step, init, unroll=max(1, min(unroll, s_len // kc)))
    o = acc / jnp.sum(lp, axis=-1, keepdims=True)
    for g in range(group):
        o_ref[0, :, g * HEAD_DIM:(g + 1) * HEAD_DIM] = o[g * tq:(g + 1) * tq].astype(o_ref.dtype)


def _grid_attention(qk, q_col, k_col, proj, *, n_heads, n_kv, v_col, tq=256, kc=512, unroll=8):
    b, s, _ = qk.shape
    group = n_heads // n_kv
    qw = group * HEAD_DIM
    kc = min(kc, s)
    assert q_col % qw == 0 and k_col % HEAD_DIM == 0 and v_col % HEAD_DIM == 0
    qb0, kb0, vb0 = q_col // qw, k_col // HEAD_DIM, v_col // HEAD_DIM
    return pl.pallas_call(
        functools.partial(_grid_attn_body, tq=tq, group=group, kc=kc, s_len=s, unroll=unroll),
        grid=(b, n_kv, s // tq),
        in_specs=[pl.BlockSpec((1, tq, qw), lambda bi, kv, i: (bi, i, qb0 + kv)),
                  pl.BlockSpec((1, s, HEAD_DIM), lambda bi, kv, i: (bi, 0, kb0 + kv)),
                  pl.BlockSpec((1, s, HEAD_DIM), lambda bi, kv, i: (bi, 0, vb0 + kv))],
        out_specs=pl.BlockSpec((1, tq, qw), lambda bi, kv, i: (bi, i, kv)),
        out_shape=jax.ShapeDtypeStruct((b, s, n_heads * HEAD_DIM), BF16),
        compiler_params=_params("parallel", "parallel", "arbitrary"),
        name="grid_attn",
    )(qk, qk, proj)


def _diff_attn_body(slope_ref, lamc_ref, sub_ref, q_ref, k_ref, v_ref, o_ref, vt_ref,
                    *, tq, kc, s_len, lambda_init, unroll):
    h = pl.program_id(1)
    r = pl.program_id(2)
    slope2 = slope_ref[h] * LOG2E
    q = q_ref[0]
    zq = jnp.zeros((tq, DIFF_DIM), q.dtype)
    qd = jnp.concatenate([jnp.concatenate([q[:, :DIFF_DIM], zq], axis=1),
                          jnp.concatenate([zq, q[:, DIFF_DIM:]], axis=1)], axis=0)
    rows = 2 * tq
    n_chunks = s_len // kc
    i0 = r * tq
    c_diag = i0 // kc
    kj = lax.broadcasted_iota(jnp.int32, (kc, 128), 0).astype(F32)
    qi = (lax.broadcasted_iota(jnp.int32, (1, rows), 1) % tq).astype(F32)

    def off_diag(c):
        return c + (c >= c_diag).astype(jnp.int32)

    def terms(cc):
        is_left = cc < c_diag
        gap = jnp.where(is_left, i0 - (cc + 1) * kc, cc * kc - i0 - tq).astype(F32)
        keyterm = -slope2 * jnp.where(is_left, kc - kj, kj)
        qterm = -slope2 * (jnp.where(is_left, qi, tq - qi) + gap)
        return [keyterm] * (rows // 128), qterm

    def diag_bias():
        in_chunk = (i0 - c_diag * kc).astype(F32)
        kfull = lax.broadcasted_iota(jnp.int32, (kc, rows), 0).astype(F32)
        return _lane_groups(-slope2 * jnp.abs(qi + in_chunk - kfull))

    def probs(cc, bias_groups, qterm, m, l):
        off = pl.multiple_of(cc * kc, kc)
        st = lax.dot_general(k_ref[0, pl.ds(off, kc), :], qd, _NT, preferred_element_type=F32)
        ug = [x + bg for x, bg in zip(_lane_groups(st), bias_groups)]
        cmax = jnp.concatenate([jnp.max(x, axis=0, keepdims=True) for x in ug], axis=1)
        m_new = jnp.maximum(m, cmax + qterm)
        alpha = jnp.exp2(m - m_new)
        shift = m_new - qterm
        ps = [jnp.exp2(x - shift[:, g * 128:(g + 1) * 128]) for g, x in enumerate(ug)]
        l = alpha * l + jnp.concatenate([jnp.sum(x, axis=0, keepdims=True) for x in ps], axis=1)
        pt = jnp.concatenate([x.astype(BF16) for x in ps], axis=1)
        return m_new, l, alpha, pt

    @pl.when(r == 0)
    def _transpose_values():
        for c in range(n_chunks):
            vt_ref[c] = v_ref[0, c * kc:(c + 1) * kc, :].astype(F32).T.astype(BF16)

    def weighted_values(cc, alpha, pt, acc):
        return alpha * acc + jnp.dot(vt_ref[cc], pt, preferred_element_type=F32)

    def step(c, carry):
        m, l, acc, alpha, pt, prev = carry
        cc = off_diag(c)
        bias_groups, qterm = terms(cc)
        m, l, alpha_next, pt_next = probs(cc, bias_groups, qterm, m, l)
        acc = weighted_values(prev, alpha, pt, acc)
        return m, l, acc, alpha_next, pt_next, cc

    m0 = jnp.full((1, rows), NEG, F32)
    m, l, alpha, pt = probs(c_diag, diag_bias(), 0.0, m0, jnp.zeros((1, rows), F32))
    carry = (m, l, jnp.zeros((2 * DIFF_DIM, rows), F32), alpha, pt, c_diag)
    _, l, acc, alpha, pt, last = lax.fori_loop(0, n_chunks - 1, step, carry,
                                               unroll=max(1, min(unroll, n_chunks - 1)))
    ot = weighted_values(last, alpha, pt, acc) / l
    lf = lamc_ref[...]
    lam = (jnp.exp(jnp.sum(lf[0:1] * lf[1:2], axis=-1, keepdims=True))
           - jnp.exp(jnp.sum(lf[2:3] * lf[3:4], axis=-1, keepdims=True)) + lambda_init)
    dt = ot[:, :tq] - lam * ot[:, tq:]
    ms = jnp.mean(dt * dt, axis=0, keepdims=True)
    o = (dt * lax.rsqrt(ms + EPS)).T * sub_ref[...] * (1.0 - lambda_init)
    o_ref[0] = o.astype(o_ref.dtype)


def _diff_attention(proj, slopes, c_lambda, c_subnorm, lambda_init, *, n_heads, tq=256, kc=512, unroll=7):
    b, s, _ = proj.shape
    hw = 2 * DIFF_DIM
    kc = min(kc, s)
    tq = min(tq, kc)
    assert kc % tq == 0 and s % kc == 0
    return pl.pallas_call(
        functools.partial(_diff_attn_body, tq=tq, kc=kc, s_len=s, lambda_init=lambda_init, unroll=unroll),
        grid=(b, n_heads, s // tq),
        in_specs=[pl.BlockSpec(memory_space=pltpu.SMEM),
                  pl.BlockSpec((4, DIFF_DIM), lambda bi, h, i: (0, 0)),
                  pl.BlockSpec((1, hw), lambda bi, h, i: (0, 0)),
                  pl.BlockSpec((1, tq, hw), lambda bi, h, i: (bi, i, h)),
                  pl.BlockSpec((1, s, hw), lambda bi, h, i: (bi, 0, n_heads + h)),
                  pl.BlockSpec((1, s, hw), lambda bi, h, i: (bi, 0, 2 * n_heads + h))],
        out_specs=pl.BlockSpec((1, tq, hw), lambda bi, h, i: (bi, i, h)),
        out_shape=jax.ShapeDtypeStruct((b, s, n_heads * hw), BF16),
        scratch_shapes=[pltpu.VMEM((s // kc, hw, kc), BF16)],
        compiler_params=_params("arbitrary", "arbitrary", "arbitrary"),
        name="diff_attn",
    )(slopes, c_lambda, c_subnorm.reshape(1, hw), proj, proj, proj)


def _router_body(h_ref, g_ref, wrt_ref, aff_ref):
    x = h_ref[...]
    ms = jnp.mean(x * x, axis=-1, keepdims=True)
    xn = x * lax.rsqrt(ms + EPS) * g_ref[...]
    logits = lax.dot_general(wrt_ref[...], xn, _NT, precision=lax.Precision.HIGHEST,
                             preferred_element_type=F32)
    m = jnp.max(logits, axis=0, keepdims=True)
    e = jnp.exp(logits - m)
    aff_ref[0] = e / jnp.sum(e, axis=0, keepdims=True)


def _router(h2d, g, w_router, b, s, tm=512):
    t, d = h2d.shape
    tm = min(tm, s)
    spb = s // tm
    ne = w_router.shape[1]
    return pl.pallas_call(
        _router_body,
        grid=(t // tm,),
        in_specs=[pl.BlockSpec((tm, d), lambda i: (i, 0)),
                  pl.BlockSpec((1, d), lambda i: (0, 0)),
                  pl.BlockSpec((ne, d), lambda i: (0, 0))],
        out_specs=pl.BlockSpec((1, ne, tm), lambda i: (i // spb, 0, i % spb)),
        out_shape=jax.ShapeDtypeStruct((b, ne, s), F32),
        compiler_params=_params("parallel"),
        name="router",
    )(h2d, g.reshape(1, d), w_router.T)


def _cumsum_lanes(mask, tri):
    rows, n = mask.shape
    carry = jnp.zeros((rows, 1), F32)
    out = []
    for blk in range(n // 128):
        c = jnp.dot(mask[:, blk * 128:(blk + 1) * 128].astype(BF16), tri, preferred_element_type=F32) + carry
        out.append(c)
        carry = c[:, 127:128]
    return jnp.concatenate(out, axis=1)


def _select_body(aff_ref, pos_ref, *, cap):
    a = aff_ref[0]
    bits = pltpu.bitcast(a, jnp.int32)
    ne = a.shape[0]
    thr = jnp.zeros((ne, 1), jnp.int32)
    for bit in range(30, -1, -1):
        cand = thr | (1 << bit)
        cnt = jnp.sum((bits >= cand).astype(jnp.int32), axis=1, keepdims=True)
        thr = jnp.where(cnt >= cap, cand, thr)
    gt = bits > thr
    eq = bits == thr
    need = (cap - jnp.sum(gt.astype(jnp.int32), axis=1, keepdims=True)).astype(F32)
    r = lax.broadcasted_iota(jnp.int32, (128, 128), 0)
    c = lax.broadcasted_iota(jnp.int32, (128, 128), 1)
    tri = (r <= c).astype(BF16)
    sel = gt | (eq & (_cumsum_lanes(eq.astype(F32), tri) <= need))
    pos = _cumsum_lanes(sel.astype(F32), tri) - 1.0
    pos_ref[0] = jnp.where(sel, pos, -1.0)


def _select(aff, cap):
    b, ne, s = aff.shape
    return pl.pallas_call(
        functools.partial(_select_body, cap=cap),
        grid=(b,),
        in_specs=[pl.BlockSpec((1, ne, s), lambda bi: (bi, 0, 0))],
        out_specs=pl.BlockSpec((1, ne, s), lambda bi: (bi, 0, 0)),
        out_shape=jax.ShapeDtypeStruct((b, ne, s), F32),
        compiler_params=_params("parallel"),
        name="select",
    )(aff)


def _compact_body(pos_ref, aff_ref, idx_ref, gate_ref, *, tc):
    c0 = pl.program_id(1) * tc
    pos = pos_ref[0]
    s = pos.shape[1]
    slot = (c0 + lax.broadcasted_iota(jnp.int32, (tc, s), 0)).astype(F32)
    tok = lax.broadcasted_iota(jnp.int32, (tc, s), 1)
    hit = pos == slot
    idx_ref[0] = jnp.sum(jnp.where(hit, tok, 0), axis=1, keepdims=True)
    gate_ref[0] = jnp.sum(jnp.where(hit, aff_ref[0], 0.0), axis=1, keepdims=True)


def _compact(pos, aff, cap, tc=128):
    b, ne, s = pos.shape
    tc = min(tc, cap)
    row = pl.BlockSpec((1, 1, s), lambda r, j: (r, 0, 0))
    col = pl.BlockSpec((1, tc, 1), lambda r, j: (r, j, 0))
    return pl.pallas_call(
        functools.partial(_compact_body, tc=tc),
        grid=(b * ne, cap // tc),
        in_specs=[row, row],
        out_specs=[col, col],
        out_shape=[jax.ShapeDtypeStruct((b * ne, cap, 1), jnp.int32),
                   jax.ShapeDtypeStruct((b * ne, cap, 1), F32)],
        compiler_params=_params("parallel", "parallel"),
        name="compact",
    )(pos.reshape(b * ne, 1, s), aff.reshape(b * ne, 1, s))


def _ffn_body(idx_ref, h_hbm, g_ref, gate_ref, wg_ref, wu_ref, wd_ref, o_ref, land, xg, hid, sems,
              *, cap, s_len, n_b, n_f, n_d):
    e = pl.program_id(0)
    step = pl.program_id(1)
    ne = pl.num_programs(0)
    rows = n_b * cap
    piece = rows // n_d
    cur = e % 2

    def row_copy(src_row, slot, r):
        return pltpu.make_async_copy(h_hbm.at[pl.ds(src_row, 1)], land.at[slot, pl.ds(r, 1)], sems.at[slot])

    def for_piece(fn, unrolled):
        if unrolled:
            for r in range(piece):
                fn(r)
        else:
            def body(r, carry):
                fn(r)
                return carry
            lax.fori_loop(0, piece, body, 0, unroll=8)

    def issue(ex, p, slot, unrolled):
        bi, c0 = (p * piece) // cap, (p * piece) % cap
        base = (bi * ne + ex) * cap + c0
        for_piece(lambda r: row_copy(bi * s_len + idx_ref[base + r], slot, r).start(), unrolled)

    def drain(slot, unrolled):
        del unrolled
        pltpu.make_async_copy(h_hbm.at[pl.ds(0, piece)], land.at[slot], sems.at[slot]).wait()

    def norm_into(slot, half, p):
        x = land[slot]
        ms = jnp.mean(x * x, axis=-1, keepdims=True)
        xg[half, pl.ds(pl.multiple_of(p * piece, piece), piece), :] = (
            x * lax.rsqrt(ms + EPS) * g_ref[...]).astype(BF16)

    @pl.when((e == 0) & (step == 0))
    def _prologue():
        issue(0, 0, 0, False)
        for p in range(n_d):
            if p + 1 < n_d:
                issue(0, p + 1, (p + 1) % 2, False)
            drain(p % 2, False)
            norm_into(p % 2, 0, p)
        issue(jnp.minimum(1, ne - 1), 0, 0, False)

    @pl.when(step < n_f)
    def _up():
        x = xg[cur]
        gt = jnp.dot(x, wg_ref[0, 0].astype(BF16), preferred_element_type=F32)
        up = jnp.dot(x, wu_ref[0, 0].astype(BF16), preferred_element_type=F32)
        hid[step] = (gt * jax.nn.sigmoid(gt) * up).astype(BF16)

    @pl.when(step >= n_f)
    def _down():
        hcat = jnp.concatenate([hid[f] for f in range(n_f)], axis=1)
        y = jnp.dot(hcat, wd_ref[0, 0].astype(BF16), preferred_element_type=F32)
        o_ref[0] = y * gate_ref[:, 0].reshape(rows, 1)
        j = step - n_f
        slot = j % 2
        drain(slot, True)
        norm_into(slot, 1 - cur, j)
        following = (j + 1) % n_d
        issue(jnp.minimum(e + 1 + (j + 1) // n_d, ne - 1), following, 1 - slot, True)

    @pl.when((e == ne - 1) & (step == n_f + n_d - 1))
    def _drain_tail():
        drain(0, False)


def _expert_ffn(idx_flat, h2d, g, gates, w_gate, w_up, w_down, layer, b, s, cap, tf=256, tn=512):
    t, d = h2d.shape
    _, ne, _, dff = w_gate.shape
    tf = min(tf, dff)
    tn = min(tn, d)
    n_f = dff // tf
    n_d = d // tn
    rows = b * cap
    piece = rows // n_d
    assert n_d % 2 == 0 and cap % piece == 0 and piece % 16 == 0
    grid_spec = pltpu.PrefetchScalarGridSpec(
        num_scalar_prefetch=1,
        grid=(ne, n_f + n_d),
        in_specs=[pl.BlockSpec(memory_space=pl.ANY),
                  pl.BlockSpec((1, d), lambda e, st, idx: (0, 0)),
                  pl.BlockSpec((b, 1, cap, 1), lambda e, st, idx: (0, e, 0, 0)),
                  pl.BlockSpec((1, 1, d, tf), lambda e, st, idx: (layer, e, 0, jnp.minimum(st, n_f - 1))),
                  pl.BlockSpec((1, 1, d, tf), lambda e, st, idx: (layer, e, 0, jnp.minimum(st, n_f - 1))),
                  pl.BlockSpec((1, 1, dff, tn), lambda e, st, idx: (layer, e, 0, jnp.maximum(st - n_f, 0)))],
        out_specs=pl.BlockSpec((1, rows, tn), lambda e, st, idx: (e, 0, jnp.maximum(st - n_f, 0))),
        scratch_shapes=[pltpu.VMEM((2, piece, d), F32), pltpu.VMEM((2, rows, d), BF16),
                        pltpu.VMEM((n_f, rows, tf), BF16), pltpu.SemaphoreType.DMA((2,))],
    )
    return pl.pallas_call(
        functools.partial(_ffn_body, cap=cap, s_len=s, n_b=b, n_f=n_f, n_d=n_d),
        grid_spec=grid_spec,
        out_shape=jax.ShapeDtypeStruct((ne, rows, d), F32),
        compiler_params=_params("arbitrary", "arbitrary"),
        name="expert_ffn",
    )(idx_flat, h2d, g.reshape(1, d), gates.reshape(b, ne, cap, 1), w_gate, w_up, w_down)


def _combine_body(idx_ref, h_hbm, y_ref, o_hbm, buf, gsem, ssem, *, cap, n_b):
    del h_hbm
    e = pl.program_id(0)
    b = pl.program_id(1)
    ne = pl.num_programs(0)
    n_steps = ne * n_b
    step = e * n_b + b
    slot = step % 2

    def gather_row(c, bi, tok, sl):
        return pltpu.make_async_copy(o_hbm.at[bi, pl.ds(tok, 1)], buf.at[sl, pl.ds(c, 1)], gsem.at[sl])

    def scatter_row(c, tok, sl):
        return pltpu.make_async_copy(buf.at[sl, pl.ds(c, 1)], o_hbm.at[b, pl.ds(tok, 1)], ssem.at[sl])

    def for_rows(fn):
        def body(c, carry):
            fn(c)
            return carry
        lax.fori_loop(0, cap, body, 0, unroll=8)

    def start_gather(st, sl):
        e2, b2 = st // n_b, st % n_b
        base2 = (b2 * ne + e2) * cap
        for_rows(lambda c: gather_row(c, b2, idx_ref[base2 + c], sl).start())

    def wait_scatter(sl):
        pltpu.make_async_copy(buf.at[sl], o_hbm.at[b, pl.ds(0, cap)], ssem.at[sl]).wait()

    pl.when(step == 0)(lambda: start_gather(0, 0))
    pltpu.make_async_copy(o_hbm.at[b, pl.ds(0, cap)], buf.at[slot], gsem.at[slot]).wait()
    buf[slot] = buf[slot] + y_ref[0]
    base = (b * ne + e) * cap
    for_rows(lambda c: scatter_row(c, idx_ref[base + c], slot).start())
    if n_b == 1:
        wait_scatter(slot)
    else:
        pl.when(step > 0)(lambda: wait_scatter(1 - slot))
    pl.when(step + 1 < n_steps)(lambda: start_gather(step + 1, 1 - slot))
    if n_b > 1:
        pl.when(step == n_steps - 1)(lambda: wait_scatter(slot))


def _combine(idx_flat, h, y, cap):
    b, s, d = h.shape
    ne = y.shape[0]
    grid_spec = pltpu.PrefetchScalarGridSpec(
        num_scalar_prefetch=1,
        grid=(ne, b),
        in_specs=[pl.BlockSpec(memory_space=pl.ANY),
                  pl.BlockSpec((1, cap, d), lambda e, bi, idx: (e, bi, 0))],
        out_specs=pl.BlockSpec(memory_space=pl.ANY),
        scratch_shapes=[pltpu.VMEM((2, cap, d), F32), pltpu.SemaphoreType.DMA((2,)),
                        pltpu.SemaphoreType.DMA((2,))],
    )
    return pl.pallas_call(
        functools.partial(_combine_body, cap=cap, n_b=b),
        grid_spec=grid_spec,
        out_shape=jax.ShapeDtypeStruct((b, s, d), F32),
        input_output_aliases={1: 0},
        compiler_params=_params("arbitrary", "arbitrary"),
        name="combine",
    )(idx_flat, h, y)


def _moe(h, g, w_router, w_gate, w_up, w_down, layer):
    b, s, d = h.shape
    ne = w_router.shape[1]
    cap = CAPACITY_FACTOR * s // ne
    h2d = h.reshape(b * s, d)
    aff = _router(h2d, g, w_router, b, s)
    pos = _select(aff, cap)
    idx, gates = _compact(pos, aff, cap)
    idx_flat = idx.reshape(b * ne * cap)
    y = _expert_ffn(idx_flat, h2d, g, gates, w_gate, w_up, w_down, layer, b, s, cap)
    return _combine(idx_flat, h, y, cap)


def _alibi_slopes(n):
    return 2.0 ** (-8.0 * jnp.arange(1, n + 1, dtype=F32) / n)


def _mm_tiles(m, n):
    tm = next(t for t in (1024, 512, 256, 128, 8) if m % t == 0)
    tn = next(t for t in (512, 256, 128) if n % t == 0)
    return tm, tn


def _even_mixer(h, g, w_in, w_out, sink, qnorm, knorm):
    b, s, d = h.shape
    t = b * s
    n_heads = d // HEAD_DIM
    ha, hb = n_heads // 2, n_heads // 2
    kva, kvb = ha // 4, hb // 4
    q_a, kv_a, q_b, kv_b = ha * HEAD_DIM, kva * HEAD_DIM, hb * HEAD_DIM, kvb * HEAD_DIM
    h2d = h.reshape(t, d)
    hn = _rmsnorm(h2d, g, BF16)
    w_total = w_in.shape[1]
    tm, tn = _mm_tiles(t, w_total)
    col_scale = jnp.where(jnp.arange(w_total) < q_a, HEAD_DIM ** -0.5 * LOG2E, 1.0).astype(F32)
    qb0 = q_a + 2 * kv_a
    gains = jnp.stack([qnorm, knorm])
    q_mult = HEAD_DIM ** -0.5 * LOG2E
    fused = tn == kv_b and qb0 % tn == 0 and s % tm == 0
    rope = (qb0 // tn, q_b // tn, q_mult, gains) + _rope_tables(s) if fused else None
    proj = _matmul([hn], w_in, None, BF16, tm, tn, "in_proj_even", col_scale, rope).reshape(b, s, w_total)
    out_a = _window_attention(proj, _alibi_slopes(ha), sink, n_heads=ha, n_kv=kva,
                              q_col=0, k_col=q_a, v_col=q_a + kv_a)
    if fused:
        qk, q_col, k_col = proj, qb0, qb0 + q_b
    else:
        qk, q_col, k_col = _qk_prep(proj, gains, q_col=qb0, n_q_heads=hb, n_k_heads=kvb), 0, q_b
    out_b = _grid_attention(qk, q_col, k_col, proj, n_heads=hb, n_kv=kvb, v_col=qb0 + q_b + kv_b)
    tm, tn = _mm_tiles(t, d)
    out = _matmul([out_a.reshape(t, q_a), out_b.reshape(t, q_b)], w_out, h2d, F32, tm, tn, "out_proj_even")
    return out.reshape(b, s, d)


def _odd_mixer(h, g, w_in, w_out, c_lambda, c_subnorm, lambda_init):
    b, s, d = h.shape
    t = b * s
    n_heads = d // (2 * DIFF_DIM)
    h2d = h.reshape(t, d)
    hn = _rmsnorm(h2d, g, BF16)
    w_total = w_in.shape[1]
    tm, tn = _mm_tiles(t, w_total)
    col_scale = jnp.where(jnp.arange(w_total) < d, DIFF_DIM ** -0.5 * LOG2E, 1.0).astype(F32)
    proj = _matmul([hn], w_in, None, BF16, tm, tn, "in_proj_odd", col_scale).reshape(b, s, w_total)
    mix = _diff_attention(proj, _alibi_slopes(n_heads), c_lambda, c_subnorm, lambda_init, n_heads=n_heads)
    tm, tn = _mm_tiles(t, d)
    out = _matmul([mix.reshape(t, d)], w_out, h2d, F32, tm, tn, "out_proj_odd")
    return out.reshape(b, s, d)


def kernel(x, norm_mix, norm_ffn, norm_final, w_in_even, w_out_even, sink_a, qnorm_b, knorm_b,
           w_in_odd, w_out_odd, c_lambda, c_subnorm, w_router, w_gate, w_up, w_down):
    b, s, d = x.shape
    depth = norm_mix.shape[0]
    h = x
    for layer in range(depth):
        i = layer // 2
        if layer % 2 == 0:
            h = _even_mixer(h, norm_mix[layer], w_in_even[i], w_out_even[i], sink_a[i], qnorm_b[i], knorm_b[i])
        else:
            lambda_init = 0.8 - 0.6 * math.exp(-0.3 * layer)
            h = _odd_mixer(h, norm_mix[layer], w_in_odd[i], w_out_odd[i], c_lambda[i], c_subnorm[i], lambda_init)
        h = _moe(h, norm_ffn[layer], w_router[layer], w_gate, w_up, w_down, layer)
    return _rmsnorm(h.reshape(b * s, d), norm_final, F32).reshape(b, s, d)
```

```python
import functools
import math

import jax
import jax.numpy as jnp
from jax import lax
from jax.experimental import pallas as pl
from jax.experimental.pallas import tpu as pltpu

F32 = jnp.float32
BF16 = jnp.bfloat16

HEAD_DIM = 128
WINDOW = 128
BLOCK = 128
GRID_W = 64
ROPE_THETA = 10000.0
DIFF_DIM = 128
N_EXPERTS = 16
CAPACITY_FACTOR = 2
EPS = 1e-6
NEG = -1e30
LOG2E = math.log2(math.e)

VMEM_LIMIT_BYTES = 56 * 1024 * 1024

_NT = (((1,), (1,)), ((), ()))


def _params(*sem):
    return pltpu.CompilerParams(dimension_semantics=sem, vmem_limit_bytes=VMEM_LIMIT_BYTES)


def _rmsnorm_body(x_ref, g_ref, o_ref):
    x = x_ref[...]
    ms = jnp.mean(x * x, axis=-1, keepdims=True)
    o_ref[...] = (x * lax.rsqrt(ms + EPS) * g_ref[...]).astype(o_ref.dtype)


def _rmsnorm(x2d, g, out_dtype, tm=256):
    t, d = x2d.shape
    return pl.pallas_call(
        _rmsnorm_body,
        grid=(t // tm,),
        in_specs=[pl.BlockSpec((tm, d), lambda i: (i, 0)),
                  pl.BlockSpec((1, d), lambda i: (0, 0))],
        out_specs=pl.BlockSpec((tm, d), lambda i: (i, 0)),
        out_shape=jax.ShapeDtypeStruct((t, d), out_dtype),
        compiler_params=_params("parallel"),
        name="rmsnorm",
    )(x2d, g.reshape(1, d))


def _rope_heads(acc, gain, mult, cos, sin):
    lane = lax.broadcasted_iota(jnp.int32, cos.shape, 1)
    first = (lane % (HEAD_DIM // 2)) < (HEAD_DIM // 4)
    out = []
    for x in _lane_groups(acc):
        ms = jnp.mean(x * x, axis=-1, keepdims=True)
        y = x * lax.rsqrt(ms + EPS) * gain
        partner = jnp.where(first, pltpu.roll(y, HEAD_DIM - HEAD_DIM // 4, 1), pltpu.roll(y, HEAD_DIM // 4, 1))
        out.append((y * cos + partner * sin) * mult)
    return jnp.concatenate(out, axis=1)


def _mm_body(*refs, n_pairs, has_res, has_scale, rope):
    n_in = 2 * n_pairs + int(has_res) + int(has_scale) + (3 if rope else 0)
    o_ref = refs[n_in]
    wbf = refs[n_in + 1:]

    @pl.when(pl.program_id(1) == 0)
    def _cast():
        for p in range(n_pairs):
            wbf[p][...] = refs[2 * p + 1][...].astype(BF16)

    acc = None
    for p in range(n_pairs):
        d = jnp.dot(refs[2 * p][...], wbf[p][...], preferred_element_type=F32)
        acc = d if acc is None else acc + d
    if has_res:
        acc = acc + refs[2 * n_pairs][...]
    if has_scale:
        acc = acc * refs[2 * n_pairs + int(has_res)][...]
    if rope is None:
        o_ref[...] = acc.astype(o_ref.dtype)
    else:
        lo, n_q, q_mult = rope
        g_ref, cos_ref, sin_ref = refs[n_in - 3:n_in]
        j = pl.program_id(0)
        is_k = j == lo + n_q
        roped = (j >= lo) & (j <= lo + n_q)

        @pl.when(roped)
        def _():
            gain = jnp.where(is_k, g_ref[1:2, :], g_ref[0:1, :])
            mult = jnp.where(is_k, 1.0, q_mult)
            o_ref[...] = _rope_heads(acc, gain, mult, cos_ref[...], sin_ref[...]).astype(o_ref.dtype)

        @pl.when(jnp.logical_not(roped))
        def _():
            o_ref[...] = acc.astype(o_ref.dtype)


def _matmul(a_list, w, res, out_dtype, tm, tn, name, col_scale=None, rope=None):
    m = a_list[0].shape[0]
    n = w.shape[1]
    in_specs, args = [], []
    kp = a_list[0].shape[1]
    assert all(a.shape == (m, kp) for a in a_list) and w.shape[0] == kp * len(a_list)
    for part, a in enumerate(a_list):
        in_specs.append(pl.BlockSpec((tm, kp), lambda j, i: (i, 0)))
        in_specs.append(pl.BlockSpec((kp, tn), functools.partial(lambda j, i, r: (r, j), r=part)))
        args += [a, w]
    if res is not None:
        in_specs.append(pl.BlockSpec((tm, tn), lambda j, i: (i, j)))
        args.append(res)
    if col_scale is not None:
        in_specs.append(pl.BlockSpec((1, tn), lambda j, i: (0, j)))
        args.append(col_scale.reshape(1, n))
    if rope is not None:
        lo, n_q, q_mult, gains, cos, sin = rope
        seq_blocks = cos.shape[0] // tm
        assert cos.shape[0] % tm == 0 and tn % HEAD_DIM == 0
        in_specs.append(pl.BlockSpec(gains.shape, lambda j, i: (0, 0)))
        in_specs.append(pl.BlockSpec((tm, HEAD_DIM), lambda j, i: (i % seq_blocks, 0)))
        in_specs.append(pl.BlockSpec((tm, HEAD_DIM), lambda j, i: (i % seq_blocks, 0)))
        args += [gains, cos, sin]
        rope = (lo, n_q, q_mult)
    return pl.pallas_call(
        functools.partial(_mm_body, n_pairs=len(a_list), has_res=res is not None,
                          has_scale=col_scale is not None, rope=rope),
        grid=(n // tn, m // tm),
        in_specs=in_specs,
        out_specs=pl.BlockSpec((tm, tn), lambda j, i: (i, j)),
        out_shape=jax.ShapeDtypeStruct((m, n), out_dtype),
        scratch_shapes=[pltpu.VMEM((kp, tn), BF16) for _ in a_list],
        compiler_params=_params("arbitrary", "arbitrary"),
        name=name,
    )(*args)


def _win_body(slope_ref, sink_ref, q_ref, kp_ref, kc_ref, kn_ref, vp_ref, vc_ref, vn_ref, o_ref,
              *, nb, group):
    kv = pl.program_id(1)
    pair = pl.program_id(2)
    tq, band = 2 * BLOCK, 4 * BLOCK
    kb = jnp.concatenate([kp_ref[0], kc_ref[0], kn_ref[0]], axis=0)
    vb = jnp.concatenate([vp_ref[0], vc_ref[0], vn_ref[0]], axis=0)
    kj = lax.broadcasted_iota(jnp.int32, (band, tq), 0)
    qi = lax.broadcasted_iota(jnp.int32, (band, tq), 1)
    arel = jnp.abs(BLOCK + qi - kj)
    kpos = (2 * pair - 1) * BLOCK + kj
    valid = (arel <= WINDOW) & (kpos >= 0) & (kpos < nb * BLOCK)
    arel_f = arel.astype(F32)
    heads = [kv * group + g for g in range(group)]
    q = q_ref[0]
    for g, h in enumerate(heads):
        bias = jnp.where(valid, -(slope_ref[h] * LOG2E) * arel_f, NEG)
        sk = sink_ref[h] * LOG2E
        s = lax.dot_general(kb, q[:, g * HEAD_DIM:(g + 1) * HEAD_DIM], _NT,
                            preferred_element_type=F32) + bias
        m = jnp.maximum(jnp.max(s, axis=0, keepdims=True), sk)
        p = jnp.exp2(s - m)
        den = jnp.sum(p, axis=0, keepdims=True) + jnp.exp2(sk - m)
        ot = lax.dot_general(vb, p.astype(BF16), (((0,), (0,)), ((), ())),
                             preferred_element_type=F32) / den
        o_ref[0, :, g * HEAD_DIM:(g + 1) * HEAD_DIM] = ot.T.astype(o_ref.dtype)


def _window_attention(proj, slopes, sink, *, n_heads, n_kv, q_col, k_col, v_col):
    b, s, _ = proj.shape
    nb = s // BLOCK
    group = n_heads // n_kv
    qw = group * HEAD_DIM
    qb0, kb0, vb0 = q_col // qw, k_col // HEAD_DIM, v_col // HEAD_DIM
    assert q_col % qw == 0 and k_col % HEAD_DIM == 0 and v_col % HEAD_DIM == 0

    assert nb % 2 == 0

    def edge_spec(col0, shift):
        def imap(bi, kv, pair):
            return (bi, jnp.clip(2 * pair + shift, 0, nb - 1), col0 + kv)
        return pl.BlockSpec((1, BLOCK, HEAD_DIM), imap)

    def pair_spec(col0):
        return pl.BlockSpec((1, 2 * BLOCK, HEAD_DIM), lambda bi, kv, pair: (bi, pair, col0 + kv))

    smem = pl.BlockSpec(memory_space=pltpu.SMEM)
    return pl.pallas_call(
        functools.partial(_win_body, nb=nb, group=group),
        grid=(b, n_kv, nb // 2),
        in_specs=[smem, smem,
                  pl.BlockSpec((1, 2 * BLOCK, qw), lambda bi, kv, pair: (bi, pair, qb0 + kv)),
                  edge_spec(kb0, -1), pair_spec(kb0), edge_spec(kb0, 2),
                  edge_spec(vb0, -1), pair_spec(vb0), edge_spec(vb0, 2)],
        out_specs=pl.BlockSpec((1, 2 * BLOCK, qw), lambda bi, kv, pair: (bi, pair, kv)),
        out_shape=jax.ShapeDtypeStruct((b, s, n_heads * HEAD_DIM), BF16),
        compiler_params=_params("parallel", "parallel", "parallel"),
        name="window_attn",
    )(slopes, sink, proj, proj, proj, proj, proj, proj, proj)


def _qkprep_body(x_ref, g_ref, cos_ref, sin_ref, o_ref, *, n_qchunks, heads_per_chunk, scale):
    j = pl.program_id(2)
    is_k = j >= n_qchunks
    gain = jnp.where(is_k, g_ref[1:2, :], g_ref[0:1, :])
    mult = jnp.where(is_k, 1.0, scale)
    cos = cos_ref[...]
    sin = sin_ref[...]
    lane = lax.broadcasted_iota(jnp.int32, cos.shape, 1)
    first = (lane % (HEAD_DIM // 2)) < (HEAD_DIM // 4)
    for hd in range(heads_per_chunk):
        x = x_ref[0, :, hd * HEAD_DIM:(hd + 1) * HEAD_DIM].astype(F32)
        ms = jnp.mean(x * x, axis=-1, keepdims=True)
        y = x * lax.rsqrt(ms + EPS) * gain
        partner = jnp.where(first, pltpu.roll(y, HEAD_DIM - HEAD_DIM // 4, 1),
                            pltpu.roll(y, HEAD_DIM // 4, 1))
        r = y * cos + partner * sin
        o_ref[0, :, hd * HEAD_DIM:(hd + 1) * HEAD_DIM] = (r * mult).astype(o_ref.dtype)


def _rope_tables(s):
    rows = s // GRID_W
    row = jnp.repeat(jnp.arange(rows), GRID_W).astype(F32)
    col = jnp.tile(jnp.arange(GRID_W), rows).astype(F32)
    quarter = HEAD_DIM // 4
    inv = ROPE_THETA ** (-jnp.arange(quarter, dtype=F32) / quarter)
    ang_r = row[:, None] * inv[None, :]
    ang_c = col[:, None] * inv[None, :]
    cos = jnp.concatenate([jnp.cos(ang_r)] * 2 + [jnp.cos(ang_c)] * 2, axis=-1)
    sin = jnp.concatenate([-jnp.sin(ang_r), jnp.sin(ang_r), -jnp.sin(ang_c), jnp.sin(ang_c)], axis=-1)
    return cos, sin


def _qk_prep(proj, gains, *, q_col, n_q_heads, n_k_heads, ts=256):
    b, s, _ = proj.shape
    cw = n_k_heads * HEAD_DIM
    assert q_col % cw == 0 and (n_q_heads * HEAD_DIM) % cw == 0
    n_qchunks = n_q_heads * HEAD_DIM // cw
    c0 = q_col // cw
    cos, sin = _rope_tables(s)
    return pl.pallas_call(
        functools.partial(_qkprep_body, n_qchunks=n_qchunks, heads_per_chunk=n_k_heads,
                          scale=HEAD_DIM ** -0.5 * LOG2E),
        grid=(b, s // ts, n_qchunks + 1),
        in_specs=[pl.BlockSpec((1, ts, cw), lambda bi, i, j: (bi, i, c0 + j)),
                  pl.BlockSpec((2, HEAD_DIM), lambda bi, i, j: (0, 0)),
                  pl.BlockSpec((ts, HEAD_DIM), lambda bi, i, j: (i, 0)),
                  pl.BlockSpec((ts, HEAD_DIM), lambda bi, i, j: (i, 0))],
        out_specs=pl.BlockSpec((1, ts, cw), lambda bi, i, j: (bi, i, j)),
        out_shape=jax.ShapeDtypeStruct((b, s, (n_qchunks + 1) * cw), BF16),
        compiler_params=_params("parallel", "parallel", "parallel"),
        name="qk_prep",
    )(proj, gains, cos, sin)


def _lane_groups(x):
    return [x[:, g * 128:(g + 1) * 128] for g in range(x.shape[1] // 128)]


def _grid_attn_body(q_ref, k_ref, v_ref, o_ref, *, tq, group, kc, s_len, unroll):
    q = q_ref[0]
    qs = jnp.concatenate([q[:, g * HEAD_DIM:(g + 1) * HEAD_DIM] for g in range(group)], axis=0)
    rows = group * tq

    def step(j, carry):
        m, lp, acc = carry
        off = pl.multiple_of(j * kc, kc)
        sg = _lane_groups(lax.dot_general(qs, k_ref[0, pl.ds(off, kc), :], _NT, preferred_element_type=F32))
        cmax = jnp.max(functools.reduce(jnp.maximum, sg), axis=-1, keepdims=True)
        m_new = jnp.maximum(m, jnp.broadcast_to(cmax, (rows, 128)))
        alpha = jnp.exp2(m - m_new)
        ps = [jnp.exp2(x - m_new) for x in sg]
        lp = alpha * lp + functools.reduce(jnp.add, ps)
        p = jnp.concatenate([pg.astype(BF16) for pg in ps], axis=1)
        acc = alpha * acc + jnp.dot(p, v_ref[0, pl.ds(off, kc), :], preferred_element_type=F32)
        return m_new, lp, acc

    assert HEAD_DIM == 128
    init = (jnp.full((rows, 128), NEG, F32), jnp.zeros((rows, 128), F32), jnp.zeros((rows, HEAD_DIM), F32))
    _, lp, acc = lax.fori_loop(0, s_len // kc, step, init, unroll=max(1, min(unroll, s_len // kc)))
    o = acc / jnp.sum(lp, axis=-1, keepdims=True)
    for g in range(group):
        o_ref[0, :, g * HEAD_DIM:(g + 1) * HEAD_DIM] = o[g * tq:(g + 1) * tq].astype(o_ref.dtype)


def _grid_attention(qk, q_col, k_col, proj, *, n_heads, n_kv, v_col, tq=256, kc=512, unroll=8):
    b, s, _ = qk.shape
    group = n_heads // n_kv
    qw = group * HEAD_DIM
    kc = min(kc, s)
    assert q_col % qw == 0 and k_col % HEAD_DIM == 0 and v_col % HEAD_DIM == 0
    qb0, kb0, vb0 = q_col // qw, k_col // HEAD_DIM, v_col // HEAD_DIM
    return pl.pallas_call(
        functools.partial(_grid_attn_body, tq=tq, group=group, kc=kc, s_len=s, unroll=unroll),
        grid=(b, n_kv, s // tq),
        in_specs=[pl.BlockSpec((1, tq, qw), lambda bi, kv, i: (bi, i, qb0 + kv)),
                  pl.BlockSpec((1, s, HEAD_DIM), lambda bi, kv, i: (bi, 0, kb0 + kv)),
                  pl.BlockSpec((1, s, HEAD_DIM), lambda bi, kv, i: (bi, 0, vb0 + kv))],
        out_specs=pl.BlockSpec((1, tq, qw), lambda bi, kv, i: (bi, i, kv)),
        out_shape=jax.ShapeDtypeStruct((b, s, n_heads * HEAD_DIM), BF16),
        compiler_params=_params("parallel", "parallel", "arbitrary"),
        name="grid_attn",
    )(qk, qk, proj)


def _diff_attn_body(slope_ref, lamc_ref, sub_ref, q_ref, k_ref, v_ref, o_ref, vt_ref,
                    *, tq, kc, s_len, lambda_init, unroll):
    h = pl.program_id(1)
    r = pl.program_id(2)
    slope2 = slope_ref[h] * LOG2E
    q = q_ref[0]
    zq = jnp.zeros((tq, DIFF_DIM), q.dtype)
    qd = jnp.concatenate([jnp.concatenate([q[:, :DIFF_DIM], zq], axis=1),
                          jnp.concatenate([zq, q[:, DIFF_DIM:]], axis=1)], axis=0)
    rows = 2 * tq
    n_chunks = s_len // kc
    i0 = r * tq
    c_diag = i0 // kc
    kj = lax.broadcasted_iota(jnp.int32, (kc, 128), 0).astype(F32)
    qi = (lax.broadcasted_iota(jnp.int32, (1, rows), 1) % tq).astype(F32)

    def off_diag(c):
        return c + (c >= c_diag).astype(jnp.int32)

    def terms(cc):
        is_left = cc < c_diag
        gap = jnp.where(is_left, i0 - (cc + 1) * kc, cc * kc - i0 - tq).astype(F32)
        keyterm = -slope2 * jnp.where(is_left, kc - kj, kj)
        qterm = -slope2 * (jnp.where(is_left, qi, tq - qi) + gap)
        return [keyterm] * (rows // 128), qterm

    def diag_bias():
        in_chunk = (i0 - c_diag * kc).astype(F32)
        kfull = lax.broadcasted_iota(jnp.int32, (kc, rows), 0).astype(F32)
        return _lane_groups(-slope2 * jnp.abs(qi + in_chunk - kfull))

    def probs(cc, bias_groups, qterm, m, l):
        off = pl.multiple_of(cc * kc, kc)
        st = lax.dot_general(k_ref[0, pl.ds(off, kc), :], qd, _NT, preferred_element_type=F32)
        ug = [x + bg for x, bg in zip(_lane_groups(st), bias_groups)]
        cmax = jnp.concatenate([jnp.max(x, axis=0, keepdims=True) for x in ug], axis=1)
        m_new = jnp.maximum(m, cmax + qterm)
        alpha = jnp.exp2(m - m_new)
        shift = m_new - qterm
        ps = [jnp.exp2(x - shift[:, g * 128:(g + 1) * 128]) for g, x in enumerate(ug)]
        l = alpha * l + jnp.concatenate([jnp.sum(x, axis=0, keepdims=True) for x in ps], axis=1)
        pt = jnp.concatenate([x.astype(BF16) for x in ps], axis=1)
        return m_new, l, alpha, pt

    @pl.when(r == 0)
    def _transpose_values():
        for c in range(n_chunks):
            vt_ref[c] = v_ref[0, c * kc:(c + 1) * kc, :].astype(F32).T.astype(BF16)

    def weighted_values(cc, alpha, pt, acc):
        return alpha * acc + jnp.dot(vt_ref[cc], pt, preferred_element_type=F32)

    def step(c, carry):
        m, l, acc, alpha, pt, prev = carry
        cc = off_diag(c)
        bias_groups, qterm = terms(cc)
        m, l, alpha_next, pt_next = probs(cc, bias_groups, qterm, m, l)
        acc = weighted_values(prev, alpha, pt, acc)
        return m, l, acc, alpha_next, pt_next, cc

    m0 = jnp.full((1, rows), NEG, F32)
    m, l, alpha, pt = probs(c_diag, diag_bias(), 0.0, m0, jnp.zeros((1, rows), F32))
    carry = (m, l, jnp.zeros((2 * DIFF_DIM, rows), F32), alpha, pt, c_diag)
    _, l, acc, alpha, pt, last = lax.fori_loop(0, n_chunks - 1, step, carry,
                                               unroll=max(1, min(unroll, n_chunks - 1)))
    ot = weighted_values(last, alpha, pt, acc) / l
    lf = lamc_ref[...]
    lam = (jnp.exp(jnp.sum(lf[0:1] * lf[1:2], axis=-1, keepdims=True))
           - jnp.exp(jnp.sum(lf[2:3] * lf[3:4], axis=-1, keepdims=True)) + lambda_init)
    dt = ot[:, :tq] - lam * ot[:, tq:]
    ms = jnp.mean(dt * dt, axis=0, keepdims=True)
    o = (dt * lax.rsqrt(ms + EPS)).T * sub_ref[...] * (1.0 - lambda_init)
    o_ref[0] = o.astype(o_ref.dtype)


def _diff_attention(proj, slopes, c_lambda, c_subnorm, lambda_init, *, n_heads, tq=512, kc=512, unroll=7):
    b, s, _ = proj.shape
    hw = 2 * DIFF_DIM
    kc = min(kc, s)
    tq = min(tq, kc)
    assert kc % tq == 0 and s % kc == 0
    return pl.pallas_call(
        functools.partial(_diff_attn_body, tq=tq, kc=kc, s_len=s, lambda_init=lambda_init, unroll=unroll),
        grid=(b, n_heads, s // tq),
        in_specs=[pl.BlockSpec(memory_space=pltpu.SMEM),
                  pl.BlockSpec((4, DIFF_DIM), lambda bi, h, i: (0, 0)),
                  pl.BlockSpec((1, hw), lambda bi, h, i: (0, 0)),
                  pl.BlockSpec((1, tq, hw), lambda bi, h, i: (bi, i, h)),
                  pl.BlockSpec((1, s, hw), lambda bi, h, i: (bi, 0, n_heads + h)),
                  pl.BlockSpec((1, s, hw), lambda bi, h, i: (bi, 0, 2 * n_heads + h))],
        out_specs=pl.BlockSpec((1, tq, hw), lambda bi, h, i: (bi, i, h)),
        out_shape=jax.ShapeDtypeStruct((b, s, n_heads * hw), BF16),
        scratch_shapes=[pltpu.VMEM((s // kc, hw, kc), BF16)],
        compiler_params=_params("arbitrary", "arbitrary", "arbitrary"),
        name="diff_attn",
    )(slopes, c_lambda, c_subnorm.reshape(1, hw), proj, proj, proj)


def _router_body(h_ref, g_ref, wrt_ref, aff_ref):
    x = h_ref[...]
    ms = jnp.mean(x * x, axis=-1, keepdims=True)
    xn = x * lax.rsqrt(ms + EPS) * g_ref[...]
    logits = lax.dot_general(wrt_ref[...], xn, _NT, precision=lax.Precision.HIGHEST,
                             preferred_element_type=F32)
    m = jnp.max(logits, axis=0, keepdims=True)
    e = jnp.exp(logits - m)
    aff_ref[0] = e / jnp.sum(e, axis=0, keepdims=True)


def _router(h2d, g, w_router, b, s, tm=512):
    t, d = h2d.shape
    tm = min(tm, s)
    spb = s // tm
    ne = w_router.shape[1]
    return pl.pallas_call(
        _router_body,
        grid=(t // tm,),
        in_specs=[pl.BlockSpec((tm, d), lambda i: (i, 0)),
                  pl.BlockSpec((1, d), lambda i: (0, 0)),
                  pl.BlockSpec((ne, d), lambda i: (0, 0))],
        out_specs=pl.BlockSpec((1, ne, tm), lambda i: (i // spb, 0, i % spb)),
        out_shape=jax.ShapeDtypeStruct((b, ne, s), F32),
        compiler_params=_params("parallel"),
        name="router",
    )(h2d, g.reshape(1, d), w_router.T)


def _cumsum_lanes(mask, tri):
    rows, n = mask.shape
    carry = jnp.zeros((rows, 1), F32)
    out = []
    for blk in range(n // 128):
        c = jnp.dot(mask[:, blk * 128:(blk + 1) * 128].astype(BF16), tri, preferred_element_type=F32) + carry
        out.append(c)
        carry = c[:, 127:128]
    return jnp.concatenate(out, axis=1)


def _select_body(aff_ref, pos_ref, *, cap):
    a = aff_ref[0]
    bits = pltpu.bitcast(a, jnp.int32)
    ne = a.shape[0]
    thr = jnp.zeros((ne, 1), jnp.int32)
    for bit in range(30, -1, -1):
        cand = thr | (1 << bit)
        cnt = jnp.sum((bits >= cand).astype(jnp.int32), axis=1, keepdims=True)
        thr = jnp.where(cnt >= cap, cand, thr)
    gt = bits > thr
    eq = bits == thr
    need = (cap - jnp.sum(gt.astype(jnp.int32), axis=1, keepdims=True)).astype(F32)
    r = lax.broadcasted_iota(jnp.int32, (128, 128), 0)
    c = lax.broadcasted_iota(jnp.int32, (128, 128), 1)
    tri = (r <= c).astype(BF16)
    sel = gt | (eq & (_cumsum_lanes(eq.astype(F32), tri) <= need))
    pos = _cumsum_lanes(sel.astype(F32), tri) - 1.0
    pos_ref[0] = jnp.where(sel, pos, -1.0)


def _select(aff, cap):
    b, ne, s = aff.shape
    return pl.pallas_call(
        functools.partial(_select_body, cap=cap),
        grid=(b,),
        in_specs=[pl.BlockSpec((1, ne, s), lambda bi: (bi, 0, 0))],
        out_specs=pl.BlockSpec((1, ne, s), lambda bi: (bi, 0, 0)),
        out_shape=jax.ShapeDtypeStruct((b, ne, s), F32),
        compiler_params=_params("parallel"),
        name="select",
    )(aff)


def _compact_body(pos_ref, aff_ref, idx_ref, gate_ref, *, tc):
    c0 = pl.program_id(1) * tc
    pos = pos_ref[0]
    s = pos.shape[1]
    slot = (c0 + lax.broadcasted_iota(jnp.int32, (tc, s), 0)).astype(F32)
    tok = lax.broadcasted_iota(jnp.int32, (tc, s), 1)
    hit = pos == slot
    idx_ref[0] = jnp.sum(jnp.where(hit, tok, 0), axis=1, keepdims=True)
    gate_ref[0] = jnp.sum(jnp.where(hit, aff_ref[0], 0.0), axis=1, keepdims=True)


def _compact(pos, aff, cap, tc=128):
    b, ne, s = pos.shape
    tc = min(tc, cap)
    row = pl.BlockSpec((1, 1, s), lambda r, j: (r, 0, 0))
    col = pl.BlockSpec((1, tc, 1), lambda r, j: (r, j, 0))
    return pl.pallas_call(
        functools.partial(_compact_body, tc=tc),
        grid=(b * ne, cap // tc),
        in_specs=[row, row],
        out_specs=[col, col],
        out_shape=[jax.ShapeDtypeStruct((b * ne, cap, 1), jnp.int32),
                   jax.ShapeDtypeStruct((b * ne, cap, 1), F32)],
        compiler_params=_params("parallel", "parallel"),
        name="compact",
    )(pos.reshape(b * ne, 1, s), aff.reshape(b * ne, 1, s))


def _ffn_body(idx_ref, h_hbm, g_ref, gate_ref, wg_ref, wu_ref, wd_ref, o_ref, land, xg, hid, sems,
              *, cap, s_len, n_b, n_f, n_d):
    e = pl.program_id(0)
    step = pl.program_id(1)
    ne = pl.num_programs(0)
    rows = n_b * cap
    piece = rows // n_d
    cur = e % 2

    def row_copy(src_row, slot, r):
        return pltpu.make_async_copy(h_hbm.at[pl.ds(src_row, 1)], land.at[slot, pl.ds(r, 1)], sems.at[slot])

    def for_piece(fn, unrolled):
        if unrolled:
            for r in range(piece):
                fn(r)
        else:
            def body(r, carry):
                fn(r)
                return carry
            lax.fori_loop(0, piece, body, 0, unroll=8)

    def issue(ex, p, slot, unrolled):
        bi, c0 = (p * piece) // cap, (p * piece) % cap
        base = (bi * ne + ex) * cap + c0
        for_piece(lambda r: row_copy(bi * s_len + idx_ref[base + r], slot, r).start(), unrolled)

    def drain(slot, unrolled):
        del unrolled
        pltpu.make_async_copy(h_hbm.at[pl.ds(0, piece)], land.at[slot], sems.at[slot]).wait()

    def norm_into(slot, half, p):
        x = land[slot]
        ms = jnp.mean(x * x, axis=-1, keepdims=True)
        xg[half, pl.ds(pl.multiple_of(p * piece, piece), piece), :] = (
            x * lax.rsqrt(ms + EPS) * g_ref[...]).astype(BF16)

    @pl.when((e == 0) & (step == 0))
    def _prologue():
        issue(0, 0, 0, False)
        for p in range(n_d):
            if p + 1 < n_d:
                issue(0, p + 1, (p + 1) % 2, False)
            drain(p % 2, False)
            norm_into(p % 2, 0, p)
        issue(jnp.minimum(1, ne - 1), 0, 0, False)

    @pl.when(step < n_f)
    def _up():
        x = xg[cur]
        gt = jnp.dot(x, wg_ref[0, 0].astype(BF16), preferred_element_type=F32)
        up = jnp.dot(x, wu_ref[0, 0].astype(BF16), preferred_element_type=F32)
        hid[step] = (gt * jax.nn.sigmoid(gt) * up).astype(BF16)

    @pl.when(step >= n_f)
    def _down():
        hcat = jnp.concatenate([hid[f] for f in range(n_f)], axis=1)
        y = jnp.dot(hcat, wd_ref[0, 0].astype(BF16), preferred_element_type=F32)
        o_ref[0] = y * gate_ref[:, 0].reshape(rows, 1)
        j = step - n_f
        slot = j % 2
        drain(slot, True)
        norm_into(slot, 1 - cur, j)
        following = (j + 1) % n_d
        issue(jnp.minimum(e + 1 + (j + 1) // n_d, ne - 1), following, 1 - slot, True)

    @pl.when((e == ne - 1) & (step == n_f + n_d - 1))
    def _drain_tail():
        drain(0, False)


def _expert_ffn(idx_flat, h2d, g, gates, w_gate, w_up, w_down, layer, b, s, cap, tf=256, tn=512):
    t, d = h2d.shape
    _, ne, _, dff = w_gate.shape
    tf = min(tf, dff)
    tn = min(tn, d)
    n_f = dff // tf
    n_d = d // tn
    rows = b * cap
    piece = rows // n_d
    assert n_d % 2 == 0 and cap % piece == 0 and piece % 16 == 0
    grid_spec = pltpu.PrefetchScalarGridSpec(
        num_scalar_prefetch=1,
        grid=(ne, n_f + n_d),
        in_specs=[pl.BlockSpec(memory_space=pl.ANY),
                  pl.BlockSpec((1, d), lambda e, st, idx: (0, 0)),
                  pl.BlockSpec((b, 1, cap, 1), lambda e, st, idx: (0, e, 0, 0)),
                  pl.BlockSpec((1, 1, d, tf), lambda e, st, idx: (layer, e, 0, jnp.minimum(st, n_f - 1))),
                  pl.BlockSpec((1, 1, d, tf), lambda e, st, idx: (layer, e, 0, jnp.minimum(st, n_f - 1))),
                  pl.BlockSpec((1, 1, dff, tn), lambda e, st, idx: (layer, e, 0, jnp.maximum(st - n_f, 0)))],
        out_specs=pl.BlockSpec((1, rows, tn), lambda e, st, idx: (e, 0, jnp.maximum(st - n_f, 0))),
        scratch_shapes=[pltpu.VMEM((2, piece, d), F32), pltpu.VMEM((2, rows, d), BF16),
                        pltpu.VMEM((n_f, rows, tf), BF16), pltpu.SemaphoreType.DMA((2,))],
    )
    return pl.pallas_call(
        functools.partial(_ffn_body, cap=cap, s_len=s, n_b=b, n_f=n_f, n_d=n_d),
        grid_spec=grid_spec,
        out_shape=jax.ShapeDtypeStruct((ne, rows, d), F32),
        compiler_params=_params("arbitrary", "arbitrary"),
        name="expert_ffn",
    )(idx_flat, h2d, g.reshape(1, d), gates.reshape(b, ne, cap, 1), w_gate, w_up, w_down)


def _combine_body(idx_ref, h_hbm, y_ref, o_hbm, buf, gsem, ssem, *, cap, n_b):
    del h_hbm
    e = pl.program_id(0)
    b = pl.program_id(1)
    ne = pl.num_programs(0)
    n_steps = ne * n_b
    step = e * n_b + b
    slot = step % 2

    def gather_row(c, bi, tok, sl):
        return pltpu.make_async_copy(o_hbm.at[bi, pl.ds(tok, 1)], buf.at[sl, pl.ds(c, 1)], gsem.at[sl])

    def scatter_row(c, tok, sl):
        return pltpu.make_async_copy(buf.at[sl, pl.ds(c, 1)], o_hbm.at[b, pl.ds(tok, 1)], ssem.at[sl])

    def for_rows(fn):
        def body(c, carry):
            fn(c)
            return carry
        lax.fori_loop(0, cap, body, 0, unroll=8)

    def start_gather(st, sl):
        e2, b2 = st // n_b, st % n_b
        base2 = (b2 * ne + e2) * cap
        for_rows(lambda c: gather_row(c, b2, idx_ref[base2 + c], sl).start())

    def wait_scatter(sl):
        pltpu.make_async_copy(buf.at[sl], o_hbm.at[b, pl.ds(0, cap)], ssem.at[sl]).wait()

    pl.when(step == 0)(lambda: start_gather(0, 0))
    pltpu.make_async_copy(o_hbm.at[b, pl.ds(0, cap)], buf.at[slot], gsem.at[slot]).wait()
    buf[slot] = buf[slot] + y_ref[0]
    base = (b * ne + e) * cap
    for_rows(lambda c: scatter_row(c, idx_ref[base + c], slot).start())
    if n_b == 1:
        wait_scatter(slot)
    else:
        pl.when(step > 0)(lambda: wait_scatter(1 - slot))
    pl.when(step + 1 < n_steps)(lambda: start_gather(step + 1, 1 - slot))
    if n_b > 1:
        pl.when(step == n_steps - 1)(lambda: wait_scatter(slot))


def _combine(idx_flat, h, y, cap):
    b, s, d = h.shape
    ne = y.shape[0]
    grid_spec = pltpu.PrefetchScalarGridSpec(
        num_scalar_prefetch=1,
        grid=(ne, b),
        in_specs=[pl.BlockSpec(memory_space=pl.ANY),
                  pl.BlockSpec((1, cap, d), lambda e, bi, idx: (e, bi, 0))],
        out_specs=pl.BlockSpec(memory_space=pl.ANY),
        scratch_shapes=[pltpu.VMEM((2, cap, d), F32), pltpu.SemaphoreType.DMA((2,)),
                        pltpu.SemaphoreType.DMA((2,))],
    )
    return pl.pallas_call(
        functools.partial(_combine_body, cap=cap, n_b=b),
        grid_spec=grid_spec,
        out_shape=jax.ShapeDtypeStruct((b, s, d), F32),
        input_output_aliases={1: 0},
        compiler_params=_params("arbitrary", "arbitrary"),
        name="combine",
    )(idx_flat, h, y)


def _moe(h, g, w_router, w_gate, w_up, w_down, layer):
    b, s, d = h.shape
    ne = w_router.shape[1]
    cap = CAPACITY_FACTOR * s // ne
    h2d = h.reshape(b * s, d)
    aff = _router(h2d, g, w_router, b, s)
    pos = _select(aff, cap)
    idx, gates = _compact(pos, aff, cap)
    idx_flat = idx.reshape(b * ne * cap)
    y = _expert_ffn(idx_flat, h2d, g, gates, w_gate, w_up, w_down, layer, b, s, cap)
    return _combine(idx_flat, h, y, cap)


def _alibi_slopes(n):
    return 2.0 ** (-8.0 * jnp.arange(1, n + 1, dtype=F32) / n)


def _mm_tiles(m, n):
    tm = next(t for t in (1024, 512, 256, 128, 8) if m % t == 0)
    tn = next(t for t in (512, 256, 128) if n % t == 0)
    return tm, tn


def _even_mixer(h, g, w_in, w_out, sink, qnorm, knorm):
    b, s, d = h.shape
    t = b * s
    n_heads = d // HEAD_DIM
    ha, hb = n_heads // 2, n_heads // 2
    kva, kvb = ha // 4, hb // 4
    q_a, kv_a, q_b, kv_b = ha * HEAD_DIM, kva * HEAD_DIM, hb * HEAD_DIM, kvb * HEAD_DIM
    h2d = h.reshape(t, d)
    hn = _rmsnorm(h2d, g, BF16)
    w_total = w_in.shape[1]
    tm, tn = _mm_tiles(t, w_total)
    col_scale = jnp.where(jnp.arange(w_total) < q_a, HEAD_DIM ** -0.5 * LOG2E, 1.0).astype(F32)
    qb0 = q_a + 2 * kv_a
    gains = jnp.stack([qnorm, knorm])
    q_mult = HEAD_DIM ** -0.5 * LOG2E
    fused = tn == kv_b and qb0 % tn == 0 and s % tm == 0
    rope = (qb0 // tn, q_b // tn, q_mult, gains) + _rope_tables(s) if fused else None
    proj = _matmul([hn], w_in, None, BF16, tm, tn, "in_proj_even", col_scale, rope).reshape(b, s, w_total)
    out_a = _window_attention(proj, _alibi_slopes(ha), sink, n_heads=ha, n_kv=kva,
                              q_col=0, k_col=q_a, v_col=q_a + kv_a)
    if fused:
        qk, q_col, k_col = proj, qb0, qb0 + q_b
    else:
        qk, q_col, k_col = _qk_prep(proj, gains, q_col=qb0, n_q_heads=hb, n_k_heads=kvb), 0, q_b
    out_b = _grid_attention(qk, q_col, k_col, proj, n_heads=hb, n_kv=kvb, v_col=qb0 + q_b + kv_b)
    tm, tn = _mm_tiles(t, d)
    out = _matmul([out_a.reshape(t, q_a), out_b.reshape(t, q_b)], w_out, h2d, F32, tm, tn, "out_proj_even")
    return out.reshape(b, s, d)


def _odd_mixer(h, g, w_in, w_out, c_lambda, c_subnorm, lambda_init):
    b, s, d = h.shape
    t = b * s
    n_heads = d // (2 * DIFF_DIM)
    h2d = h.reshape(t, d)
    hn = _rmsnorm(h2d, g, BF16)
    w_total = w_in.shape[1]
    tm, tn = _mm_tiles(t, w_total)
    col_scale = jnp.where(jnp.arange(w_total) < d, DIFF_DIM ** -0.5 * LOG2E, 1.0).astype(F32)
    proj = _matmul([hn], w_in, None, BF16, tm, tn, "in_proj_odd", col_scale).reshape(b, s, w_total)
    mix = _diff_attention(proj, _alibi_slopes(n_heads), c_lambda, c_subnorm, lambda_init, n_heads=n_heads)
    tm, tn = _mm_tiles(t, d)
    out = _matmul([mix.reshape(t, d)], w_out, h2d, F32, tm, tn, "out_proj_odd")
    return out.reshape(b, s, d)


def kernel(x, norm_mix, norm_ffn, norm_final, w_in_even, w_out_even, sink_a, qnorm_b, knorm_b,
           w_in_odd, w_out_odd, c_lambda, c_subnorm, w_router, w_gate, w_up, w_down):
    b, s, d = x.shape
    depth = norm_mix.shape[0]
    h = x
    for layer in range(depth):
        i = layer // 2
        if layer % 2 == 0:
            h = _even_mixer(h, norm_mix[layer], w_in_even[i], w_out_even[i], sink_a[i], qnorm_b[i], knorm_b[i])
        else:
            lambda_init = 0.8 - 0.6 * math.exp(-0.3 * layer)
            h = _odd_mixer(h, norm_mix[layer], w_in_odd[i], w_out_odd[i], c_lambda[i], c_subnorm[i], lambda_init)
        h = _moe(h, norm_ffn[layer], w_router[layer], w_gate, w_up, w_down, layer)
    return _rmsnorm(h.reshape(b * s, d), norm_final, F32).reshape(b, s, d)
```
